```python
import jax, jax.numpy as jnp
from jax import lax
import numpy as np

D_MODEL = 1024
BATCH = 8
SEQ = 4096
DEPTH = 4
DEC_BATCH = 32
DEC_SEQ = 16
PAST_LEN = 1024

CHUNK = 64
N_A_LAYERS = DEPTH // 2
N_B_LAYERS = DEPTH - N_A_LAYERS
EPS = 1e-6
GLA_HEADS = 4
GLA_DK = D_MODEL // 2 // GLA_HEADS
GLA_DV = D_MODEL // GLA_HEADS
GLA_GATE_RANK = 16
GLA_TAU = 16.0
GLA_BLOCK = 64
GLA_IN = 2 * GLA_HEADS * GLA_DK + 2 * GLA_HEADS * GLA_DV + GLA_GATE_RANK
ATT_HEADS = 16
ATT_DH = D_MODEL // ATT_HEADS
N_PAST_CHUNKS = 8
BAND_PAST = N_PAST_CHUNKS * CHUNK
BAND = BAND_PAST + CHUNK
REL_CLIP = 128
NEG_INF = -1e30
D_FF = -(-8 * D_MODEL // (3 * 256)) * 256

kernel_name = "yoco_gla_chunkband_stream_step"


def _rmsnorm(x, g):
    xf = x.astype(jnp.float32)
    y = xf * lax.rsqrt(jnp.mean(xf * xf, axis=-1, keepdims=True) + EPS)
    return (y * g.astype(jnp.float32)).astype(x.dtype)


def _swiglu(h, wg, wu, wd):
    return (jax.nn.silu(h @ wg) * (h @ wu)) @ wd


def _gla_block(q, k, v, la, s):
    L = q.shape[1]
    b = jnp.cumsum(la, axis=1)
    qd = q * jnp.exp(b)
    kd = k * jnp.exp(-b)
    a = jnp.einsum('blhk,bshk->bhls', qd, kd)
    a = jnp.where(jnp.tril(jnp.ones((L, L), dtype=bool)), a, 0.0)
    o = jnp.einsum('blhk,bhkv->blhv', qd, s) + jnp.einsum('bhls,bshv->blhv', a, v)
    b_last = b[:, -1]
    s_new = s * jnp.exp(b_last)[..., None] + jnp.einsum(
        'bshk,bshv->bhkv', k * jnp.exp(b_last[:, None] - b), v)
    return o, s_new


def _gla_mixer(hn, s0, w_in, w_gate2, b_gate, norm_g, w_out):
    B, T, _ = hn.shape
    nk, nv = GLA_HEADS * GLA_DK, GLA_HEADS * GLA_DV
    z = hn @ w_in
    q, k, v, g, a = jnp.split(z, [nk, 2 * nk, 2 * nk + nv, 2 * nk + 2 * nv], axis=-1)
    q = q.reshape(B, T, GLA_HEADS, GLA_DK).astype(jnp.float32) * (GLA_DK ** -0.5)
    k = k.reshape(B, T, GLA_HEADS, GLA_DK).astype(jnp.float32)
    v = v.reshape(B, T, GLA_HEADS, GLA_DV).astype(jnp.float32)
    logit = (a @ w_gate2 + b_gate).astype(jnp.float32)
    la = (jax.nn.log_sigmoid(logit) / GLA_TAU).reshape(B, T, GLA_HEADS, GLA_DK)
    blk = GLA_BLOCK if T % GLA_BLOCK == 0 else T
    nb = T // blk

    def to_blocks(t):
        return t.reshape(B, nb, blk, *t.shape[2:]).swapaxes(0, 1)

    def body(s, xs):
        o_b, s = _gla_block(xs[0], xs[1], xs[2], xs[3], s)
        return s, o_b

    s_fin, o = lax.scan(body, s0, (to_blocks(q), to_blocks(k), to_blocks(v), to_blocks(la)))
    o = o.swapaxes(0, 1).reshape(B, T, GLA_HEADS, GLA_DV)
    o = o * lax.rsqrt(jnp.mean(o * o, axis=-1, keepdims=True) + EPS)
    o = o * norm_g.astype(jnp.float32).reshape(GLA_HEADS, GLA_DV)
    o = o.reshape(B, T, nv).astype(hn.dtype)
    return (o * jax.nn.silu(g)) @ w_out, s_fin


def _shared_kv(h, kv_norm, w_kv):
    B, T, _ = h.shape
    z = _rmsnorm(h, kv_norm) @ w_kv
    k, v = jnp.split(z, 2, axis=-1)
    return (k.reshape(B, T, ATT_HEADS, ATT_DH), v.reshape(B, T, ATT_HEADS, ATT_DH))


def _rel_bias(table, n_q, n_k, offset):
    a = jnp.arange(n_q)[:, None]
    j = jnp.arange(n_k)[None, :]
    idx = jnp.clip(offset + a - j, -REL_CLIP, REL_CLIP) + REL_CLIP
    return table[:, idx].astype(jnp.float32)


def _attend(q, k, v, bias, valid):
    s = jnp.einsum('bqhd,bkhd->bhqk', q, k, preferred_element_type=jnp.float32)
    s = s * (ATT_DH ** -0.5) + bias[None]
    if valid is not None:
        s = jnp.where(valid, s, NEG_INF)
    p = jax.nn.softmax(s, axis=-1)
    return jnp.einsum('bhqk,bkhd->bqhd', p.astype(v.dtype), v)


def _band_prompt(hn, k, v, w_q, table, w_out):
    B, T, _ = hn.shape
    nc = T // CHUNK
    q = (hn @ w_q).reshape(B, T, ATT_HEADS, ATT_DH)
    pad = ((0, 0), (BAND_PAST, 0), (0, 0), (0, 0))
    kp, vp = jnp.pad(k, pad), jnp.pad(v, pad)
    bias = _rel_bias(table, CHUNK, BAND, BAND_PAST)
    r = jnp.arange(BAND)

    def one_chunk(c):
        start = c * CHUNK
        qc = lax.dynamic_slice_in_dim(q, start, CHUNK, axis=1)
        kc = lax.dynamic_slice_in_dim(kp, start, BAND, axis=1)
        vc = lax.dynamic_slice_in_dim(vp, start, BAND, axis=1)
        valid = (r >= BAND_PAST - start)[None, None, None, :]
        return _attend(qc, kc, vc, bias, valid)

    o = lax.map(one_chunk, jnp.arange(nc))
    o = o.swapaxes(0, 1).reshape(B, T, ATT_HEADS * ATT_DH)
    return o @ w_out


def _band_sample(hn, k_all, v_all, n_past, w_q, table, w_out):
    B, L, _ = hn.shape
    q = (hn @ w_q).reshape(B, L, ATT_HEADS, ATT_DH)
    bias = _rel_bias(table, L, n_past + L, n_past)
    o = _attend(q, k_all, v_all, bias, None)
    return o.reshape(B, L, ATT_HEADS * ATT_DH) @ w_out


def _trunk(x, state, ck, cv, prompt, norm_mix, norm_ffn, norm_final, gla_w_in, gla_w_gate2,
           gla_b_gate, gla_norm, gla_w_out, kv_norm, w_kv, attn_w_q, attn_rel_bias,
           attn_w_out, ffn_w_gate, ffn_w_up, ffn_w_down):
    bsz, T, _ = x.shape
    h = x
    new_s = []
    for l in range(DEPTH):
        hn = _rmsnorm(h, norm_mix[l])
        if l < N_A_LAYERS:
            if prompt:
                s0 = jnp.zeros((bsz, GLA_HEADS, GLA_DK, GLA_DV), jnp.float32)
            else:
                s0 = state[l].astype(jnp.float32)
            o, s1 = _gla_mixer(hn, s0, gla_w_in[l], gla_w_gate2[l], gla_b_gate[l],
                               gla_norm[l], gla_w_out[l])
            new_s.append(s1.astype(x.dtype))
        else:
            if l == N_A_LAYERS:
                k, v = _shared_kv(h, kv_norm, w_kv)
                if prompt:
                    keep = min(BAND_PAST, T)
                    new_k, new_v = k[:, T - keep:], v[:, T - keep:]
                else:
                    n_past = ck.shape[1]
                    k_all = jnp.concatenate([ck.astype(k.dtype), k], axis=1)
                    v_all = jnp.concatenate([cv.astype(v.dtype), v], axis=1)
                    new_k, new_v = k_all[:, T:], v_all[:, T:]
            i = l - N_A_LAYERS
            if prompt:
                o = _band_prompt(hn, k, v, attn_w_q[i], attn_rel_bias[i], attn_w_out[i])
            else:
                o = _band_sample(hn, k_all, v_all, n_past, attn_w_q[i], attn_rel_bias[i],
                                 attn_w_out[i])
        h = h + o
        h = h + _swiglu(_rmsnorm(h, norm_ffn[l]), ffn_w_gate[l], ffn_w_up[l], ffn_w_down[l])
    return _rmsnorm(h, norm_final), jnp.stack(new_s), new_k, new_v


def setup_inputs(seed: int = 0) -> dict:
    key = jax.random.key(seed)
    ks = jax.random.split(key, 24)
    f32 = jnp.float32
    win = min(BAND_PAST, PAST_LEN)
    nrm = lambda k, shape, s: jax.random.normal(k, shape, f32) * s
    return {
        "x_prompt": nrm(ks[0], (BATCH, SEQ, D_MODEL), 1.0),
        "x_sample": nrm(ks[1], (DEC_BATCH, DEC_SEQ, D_MODEL), 1.0),
        "state_gla": nrm(ks[2], (N_A_LAYERS, DEC_BATCH, GLA_HEADS, GLA_DK, GLA_DV), 1.0),
        "cache_k": nrm(ks[3], (DEC_BATCH, win, ATT_HEADS, ATT_DH), 1.0),
        "cache_v": nrm(ks[4], (DEC_BATCH, win, ATT_HEADS, ATT_DH), 1.0),
        "norm_mix": 1.0 + nrm(ks[5], (DEPTH, D_MODEL), 0.02),
        "norm_ffn": 1.0 + nrm(ks[6], (DEPTH, D_MODEL), 0.02),
        "norm_final": 1.0 + nrm(ks[7], (D_MODEL,), 0.02),
        "gla_w_in": nrm(ks[8], (N_A_LAYERS, D_MODEL, GLA_IN), D_MODEL ** -0.5),
        "gla_w_gate2": nrm(ks[9], (N_A_LAYERS, GLA_GATE_RANK, GLA_HEADS * GLA_DK), GLA_GATE_RANK ** -0.5),
        "gla_b_gate": nrm(ks[10], (N_A_LAYERS, GLA_HEADS * GLA_DK), 0.1),
        "gla_norm": 1.0 + nrm(ks[11], (N_A_LAYERS, GLA_HEADS * GLA_DV), 0.02),
        "gla_w_out": nrm(ks[12], (N_A_LAYERS, GLA_HEADS * GLA_DV, D_MODEL), (GLA_HEADS * GLA_DV) ** -0.5),
        "kv_norm": 1.0 + nrm(ks[13], (D_MODEL,), 0.02),
        "w_kv": nrm(ks[14], (D_MODEL, 2 * ATT_HEADS * ATT_DH), D_MODEL ** -0.5),
        "attn_w_q": nrm(ks[15], (N_B_LAYERS, D_MODEL, ATT_HEADS * ATT_DH), D_MODEL ** -0.5),
        "attn_rel_bias": nrm(ks[16], (N_B_LAYERS, ATT_HEADS, 2 * REL_CLIP + 1), 0.5),
        "attn_w_out": nrm(ks[17], (N_B_LAYERS, ATT_HEADS * ATT_DH, D_MODEL), (ATT_HEADS * ATT_DH) ** -0.5),
        "ffn_w_gate": nrm(ks[18], (DEPTH, D_MODEL, D_FF), D_MODEL ** -0.5),
        "ffn_w_up": nrm(ks[19], (DEPTH, D_MODEL, D_FF), D_MODEL ** -0.5),
        "ffn_w_down": nrm(ks[20], (DEPTH, D_FF, D_MODEL), D_FF ** -0.5),
    }


def reference(x_prompt, x_sample, state_gla, cache_k, cache_v, norm_mix, norm_ffn, norm_final,
              gla_w_in, gla_w_gate2, gla_b_gate, gla_norm, gla_w_out, kv_norm, w_kv,
              attn_w_q, attn_rel_bias, attn_w_out, ffn_w_gate, ffn_w_up, ffn_w_down):
    y_prompt, state_gla_prompt, cache_k_prompt, cache_v_prompt = _trunk(
        x_prompt, None, None, None, True, norm_mix, norm_ffn, norm_final, gla_w_in,
        gla_w_gate2, gla_b_gate, gla_norm, gla_w_out, kv_norm, w_kv, attn_w_q,
        attn_rel_bias, attn_w_out, ffn_w_gate, ffn_w_up, ffn_w_down)
    y_sample, state_gla_sample, cache_k_sample, cache_v_sample = _trunk(
        x_sample, state_gla, cache_k, cache_v, False, norm_mix, norm_ffn, norm_final,
        gla_w_in, gla_w_gate2, gla_b_gate, gla_norm, gla_w_out, kv_norm, w_kv, attn_w_q,
        attn_rel_bias, attn_w_out, ffn_w_gate, ffn_w_up, ffn_w_down)
    return (y_prompt, y_sample, state_gla_prompt, cache_k_prompt, cache_v_prompt,
            state_gla_sample, cache_k_sample, cache_v_sample)
```

```python
import functools

import jax
import jax.numpy as jnp
from jax import lax
from jax.experimental import pallas as pl
from jax.experimental.pallas import tpu as pltpu

F32 = jnp.float32
BF16 = jnp.bfloat16

EPS = 1e-6
CHUNK = 64
N_PAST_CHUNKS = 8
BAND_PAST = N_PAST_CHUNKS * CHUNK
BAND = BAND_PAST + CHUNK
NEG_INF = -1e30
GLA_TAU = 16.0
GLA_BLOCK = 64
LANES = 128
ROLL_WIDTH = 640
VMEM_LIMIT_BYTES = 56 * 1024 * 1024


def _params(*sem):
    return pltpu.CompilerParams(dimension_semantics=sem, vmem_limit_bytes=VMEM_LIMIT_BYTES)


def _resident(shape):
    nd = len(shape)
    return pl.BlockSpec(shape, lambda *_: (0,) * nd, pipeline_mode=pl.Buffered(1))


def _rms(x, g):
    ms = jnp.mean(x * x, axis=-1, keepdims=True)
    return x * lax.rsqrt(ms + EPS) * g


def _dot(a, b):
    return jnp.dot(a, b, preferred_element_type=F32)


def _dot_nt(a, b):
    return lax.dot_general(a, b, (((1,), (1,)), ((), ())), preferred_element_type=F32)


def _dot_tn(a, b):
    return lax.dot_general(a, b, (((0,), (0,)), ((), ())), preferred_element_type=F32)


def _split3(x):
    hi = x.astype(BF16)
    r1 = x - hi.astype(F32)
    mid = r1.astype(BF16)
    lo = (r1 - mid.astype(F32)).astype(BF16)
    return hi, mid, lo


def _log_sigmoid(x):
    return jnp.minimum(x, 0.0) - jnp.log(1.0 + jnp.exp(-jnp.abs(x)))


def _ffn_kernel(h_ref, g_ref, wg_ref, wu_ref, wd_ref, gf_ref, o_ref, hh_ref, *, tf, final):
    h = h_ref[...]
    xn = _rms(h, g_ref[...]).astype(BF16)
    n_ff = wg_ref.shape[1]
    for c in range(n_ff // tf):
        sl = slice(c * tf, (c + 1) * tf)
        a = _dot(xn, wg_ref[:, sl])
        u = _dot(xn, wu_ref[:, sl])
        hh_ref[:, sl] = (a * jax.nn.sigmoid(a) * u).astype(BF16)
    y = h + _dot(hh_ref[...], wd_ref[...])
    if final:
        y = _rms(y, gf_ref[...])
    o_ref[...] = y


def _ffn(h, g, wg, wu, wd, gf, *, final, tm=512, tf=256):
    n, d = h.shape
    n_ff = wg.shape[1]
    tm = min(tm, n)
    assert n % tm == 0 and n_ff % tf == 0
    return pl.pallas_call(
        functools.partial(_ffn_kernel, tf=tf, final=final),
        grid=(n // tm,),
        in_specs=[
            pl.BlockSpec((tm, d), lambda i: (i, 0)),
            _resident((1, d)),
            _resident((d, n_ff)),
            _resident((d, n_ff)),
            _resident((n_ff, d)),
            _resident((1, d)),
        ],
        out_specs=pl.BlockSpec((tm, d), lambda i: (i, 0)),
        out_shape=jax.ShapeDtypeStruct((n, d), F32),
        scratch_shapes=[pltpu.VMEM((tm, n_ff), BF16)],
        compiler_params=_params("arbitrary"),
        name="ffn",
    )(h, g, wg, wu, wd, gf)


def _gla_kernel(*refs, blk, heads, dk, dv, chained):
    if chained:
        (h_ref, gm_ref, win_ref, wg2_ref, bg_ref, gn_ref, wout_ref,
         o_ref, s_ref, oacc_ref) = refs
        s_in_ref = s_ref
    else:
        (h_ref, gm_ref, win_ref, wg2_ref, bg_ref, gn_ref, wout_ref, s_in_ref,
         o_ref, s_ref, oacc_ref) = refs
    h = h_ref[0]
    tt = h.shape[0]
    nk, nv = heads * dk, heads * dv
    nblk = tt // blk

    if chained:
        @pl.when(pl.program_id(1) == 0)
        def _():
            s_ref[...] = jnp.zeros_like(s_ref)

    xn = _rms(h, gm_ref[...]).astype(BF16)
    z = _dot(xn, win_ref[...])
    q = z[:, :nk] * (dk ** -0.5)
    k = z[:, nk:2 * nk]
    v = z[:, 2 * nk:2 * nk + nv].astype(BF16)
    g = z[:, 2 * nk + nv:2 * nk + 2 * nv]
    a = z[:, 2 * nk + 2 * nv:].astype(BF16)
    logit = _dot(a, wg2_ref[...]) + bg_ref[...]
    la = _log_sigmoid(logit) * (1.0 / GLA_TAU)

    shift = blk.bit_length() - 1
    r = lax.broadcasted_iota(jnp.int32, (tt, tt), 0)
    c = lax.broadcasted_iota(jnp.int32, (tt, tt), 1)
    same = lax.shift_right_logical(r, shift) == lax.shift_right_logical(c, shift)
    m_cum = jnp.where(same & (c <= r), 1.0, 0.0).astype(BF16)
    m_tot = jnp.where(same, 1.0, 0.0).astype(BF16)
    parts = _split3(la)
    b = sum(_dot(m_cum, p) for p in parts)
    bt = sum(_dot(m_tot, p) for p in parts)

    qd = (q * jnp.exp(b)).astype(BF16)
    kd = (k * jnp.exp(-b)).astype(BF16)
    kdec = (k * jnp.exp(bt - b)).astype(BF16)
    eb = jnp.exp(bt)

    tril = (lax.broadcasted_iota(jnp.int32, (blk, blk), 1)
            <= lax.broadcasted_iota(jnp.int32, (blk, blk), 0))

    for hd in range(heads):
        kc = slice(hd * dk, (hd + 1) * dk)
        vc = slice(hd * dv, (hd + 1) * dv)
        s = s_in_ref[0, hd] if chained else None
        for j in range(nblk):
            rows = slice(j * blk, (j + 1) * blk)
            if not chained:
                s = s_in_ref[j, hd]
            qd_b, kd_b, kdec_b, v_b = qd[rows, kc], kd[rows, kc], kdec[rows, kc], v[rows, vc]
            att = jnp.where(tril, _dot_nt(qd_b, kd_b), 0.0).astype(BF16)
            oacc_ref[rows, vc] = _dot(qd_b, s.astype(BF16)) + _dot(att, v_b)
            eb_col = jnp.transpose(jnp.broadcast_to(eb[j * blk:j * blk + 1, kc], (dk, dk)))
            decay = jnp.concatenate([eb_col] * (dv // dk), axis=1)
            s = s * decay + _dot_tn(kdec_b, v_b)
            if not chained:
                s_ref[j, hd] = s
        if chained:
            s_ref[0, hd] = s

    o = oacc_ref[...]
    gn = gn_ref[...]
    ys = []
    for hd in range(heads):
        vc = slice(hd * dv, (hd + 1) * dv)
        oh = o[:, vc]
        ms = jnp.mean(oh * oh, axis=-1, keepdims=True)
        ys.append(oh * lax.rsqrt(ms + EPS) * gn[:, vc])
    y = jnp.concatenate(ys, axis=1)
    y = (y * (g * jax.nn.sigmoid(g))).astype(BF16)
    o_ref[0] = h + _dot(y, wout_ref[...])


def _gla_layer(h, s0, gm, win, wg2, bg, gn, wout, *, blk, tt, chained):
    bsz, t, d = h.shape
    heads, dk, dv = s0[1:] if chained else s0.shape[1:]
    nblk = tt // blk
    assert t % tt == 0 and tt % blk == 0
    weights = [gm, win, wg2, bg, gn, wout]
    in_specs = [pl.BlockSpec((1, tt, d), lambda b, i: (b, i, 0))]
    in_specs += [_resident(w.shape) for w in weights]
    args = [h] + weights
    if chained:
        nseq = bsz
        s_spec = pl.BlockSpec((1, heads, dk, dv), lambda b, i: (b, 0, 0, 0))
    else:
        nseq = bsz * (t // blk)
        s_spec = pl.BlockSpec((nblk, heads, dk, dv), lambda b, i: (b * (t // tt) + i, 0, 0, 0))
        in_specs.append(s_spec)
        args.append(s0)
    return pl.pallas_call(
        functools.partial(_gla_kernel, blk=blk, heads=heads, dk=dk, dv=dv, chained=chained),
        grid=(bsz, t // tt),
        in_specs=in_specs,
        out_specs=[pl.BlockSpec((1, tt, d), lambda b, i: (b, i, 0)), s_spec],
        out_shape=[jax.ShapeDtypeStruct((bsz, t, d), F32),
                   jax.ShapeDtypeStruct((nseq, heads, dk, dv), F32)],
        scratch_shapes=[pltpu.VMEM((tt, heads * dv), F32)],
        compiler_params=_params("arbitrary", "arbitrary"),
        name="gla_layer",
    )(*args)


def _norm_proj_kernel(h_ref, g_ref, w_ref, o_ref, *, scale):
    xn = _rms(h_ref[...], g_ref[...]).astype(BF16)
    y = _dot(xn, w_ref[...])
    if scale != 1.0:
        y = y * scale
    o_ref[...] = y.astype(o_ref.dtype)


def _norm_proj(h, g, w, *, scale=1.0, out_dtype=F32, tm=512):
    n, d = h.shape
    tm = min(tm, n)
    assert n % tm == 0
    return pl.pallas_call(
        functools.partial(_norm_proj_kernel, scale=scale),
        grid=(n // tm,),
        in_specs=[pl.BlockSpec((tm, d), lambda i: (i, 0)), _resident((1, d)), _resident(w.shape)],
        out_specs=pl.BlockSpec((tm, w.shape[1]), lambda i: (i, 0)),
        out_shape=jax.ShapeDtypeStruct((n, w.shape[1]), out_dtype),
        compiler_params=_params("arbitrary"),
        name="norm_proj",
    )(h, g, w)


def _proj_res_kernel(x_ref, w_ref, h_ref, o_ref):
    o_ref[...] = h_ref[...] + _dot(x_ref[...], w_ref[...])


def _proj_res(x, w, h, *, tm=512):
    n, d = h.shape
    tm = min(tm, n)
    assert n % tm == 0
    return pl.pallas_call(
        _proj_res_kernel,
        grid=(n // tm,),
        in_specs=[pl.BlockSpec((tm, x.shape[1]), lambda i: (i, 0)), _resident(w.shape),
                  pl.BlockSpec((tm, d), lambda i: (i, 0))],
        out_specs=pl.BlockSpec((tm, d), lambda i: (i, 0)),
        out_shape=jax.ShapeDtypeStruct((n, d), F32),
        compiler_params=_params("arbitrary"),
        name="proj_res",
    )(x, w, h)


def _kv_prompt_kernel(h_ref, g_ref, w_ref, kp_ref, vp_ref, kl_ref, vl_ref):
    j = pl.program_id(1)
    hk = kp_ref.shape[2]

    @pl.when(j == 0)
    def _():
        kp_ref[...] = jnp.zeros_like(kp_ref)
        vp_ref[...] = jnp.zeros_like(vp_ref)

    @pl.when(j > 0)
    def _():
        xn = _rms(h_ref[0], g_ref[...]).astype(BF16)
        z = _dot(xn, w_ref[...])
        kp_ref[0] = z[:, :hk].astype(BF16)
        vp_ref[0] = z[:, hk:].astype(BF16)

        @pl.when(j == pl.num_programs(1) - 1)
        def _():
            kl_ref[0] = z[:, :hk]
            vl_ref[0] = z[:, hk:]


def _kv_prompt(h, g, w):
    bsz, t, d = h.shape
    hk = w.shape[1] // 2
    tt = BAND_PAST
    assert t % tt == 0
    nt = t // tt
    pad_spec = pl.BlockSpec((1, tt, hk), lambda b, j: (b, j, 0))
    last_spec = pl.BlockSpec((1, tt, hk), lambda b, j: (b, 0, 0))
    return pl.pallas_call(
        _kv_prompt_kernel,
        grid=(bsz, nt + 1),
        in_specs=[pl.BlockSpec((1, tt, d), lambda b, j: (b, jnp.maximum(j - 1, 0), 0)),
                  _resident((1, d)), _resident(w.shape)],
        out_specs=[pad_spec, pad_spec, last_spec, last_spec],
        out_shape=[jax.ShapeDtypeStruct((bsz, t + tt, hk), BF16)] * 2
                  + [jax.ShapeDtypeStruct((bsz, tt, hk), F32)] * 2,
        compiler_params=_params("arbitrary", "arbitrary"),
        name="kv_prompt",
    )(h, g, w)


def _toeplitz_bias(row, nq, nk):
    full = pltpu.roll(jnp.broadcast_to(row, (nq, ROLL_WIDTH)), 0, 1, stride=1, stride_axis=0)
    return full[:, :nk]


def _attend_pair(qp, kw, vw, bias_a, bias_b, thr):
    nq, nk = qp.shape[0], kw.shape[0]
    lane = lax.broadcasted_iota(jnp.int32, (nq, LANES), 1)
    first = lane < (LANES // 2)
    out = None
    for sel, bias in ((first, bias_a), (~first, bias_b)):
        qm = jnp.where(sel, qp, jnp.zeros_like(qp))
        s = _dot_nt(qm, kw) + bias
        if thr is not None:
            col = lax.broadcasted_iota(jnp.int32, (nq, nk), 1)
            s = jnp.where(col >= thr, s, NEG_INF)
        m = jnp.max(s, axis=-1, keepdims=True)
        e = jnp.exp(s - m)
        l = jnp.sum(e, axis=-1, keepdims=True)
        pv = _dot(e.astype(BF16), vw) / l
        out = pv if out is None else jnp.where(first, out, pv)
    return out


def _band_prompt_kernel(h_ref, g_ref, wq_ref, rrow_ref, wout_ref, k_ref, v_ref, o_ref,
                        bias_ref, q_ref, oacc_ref, *, heads):
    t = pl.program_id(1)
    tq = h_ref.shape[1]
    nchunks = tq // CHUNK

    @pl.when((pl.program_id(0) == 0) & (t == 0))
    def _():
        for hd in range(heads):
            bias_ref[hd] = _toeplitz_bias(rrow_ref[hd:hd + 1, :], CHUNK, BAND)

    h = h_ref[0]
    xn = _rms(h, g_ref[...]).astype(BF16)
    dh = wq_ref.shape[1] // heads
    q_ref[...] = (_dot(xn, wq_ref[...]) * (dh ** -0.5)).astype(BF16)

    def chunk_body(i, carry):
        r0 = pl.multiple_of(i * CHUNK, CHUNK)
        thr = BAND_PAST - (t * nchunks + i) * CHUNK
        for p in range(heads // 2):
            cols = slice(p * LANES, (p + 1) * LANES)
            out = _attend_pair(q_ref[pl.ds(r0, CHUNK), cols],
                               k_ref[0, pl.ds(r0, BAND), cols],
                               v_ref[0, pl.ds(r0, BAND), cols],
                               bias_ref[2 * p], bias_ref[2 * p + 1], thr)
            oacc_ref[pl.ds(r0, CHUNK), cols] = out.astype(BF16)
        return carry

    lax.fori_loop(0, nchunks, chunk_body, 0)
    o_ref[0] = h + _dot(oacc_ref[...], wout_ref[...])


def _band_prompt(h, kpad, vpad, g, wq, rrow, wout, *, heads, tq=512):
    bsz, t, d = h.shape
    hk = wq.shape[1]
    assert t % tq == 0 and tq % CHUNK == 0 and 2 * (hk // heads) == LANES
    win = tq + BAND_PAST
    win_spec = pl.BlockSpec((pl.Element(1), pl.Element(win), pl.Element(hk)),
                            lambda b, i: (b, i * tq, 0))
    return pl.pallas_call(
        functools.partial(_band_prompt_kernel, heads=heads),
        grid=(bsz, t // tq),
        in_specs=[pl.BlockSpec((1, tq, d), lambda b, i: (b, i, 0)),
                  _resident((1, d)), _resident(wq.shape), _resident(rrow.shape),
                  _resident(wout.shape), win_spec, win_spec],
        out_specs=pl.BlockSpec((1, tq, d), lambda b, i: (b, i, 0)),
        out_shape=jax.ShapeDtypeStruct((bsz, t, d), F32),
        scratch_shapes=[pltpu.VMEM((heads, CHUNK, BAND), F32),
                        pltpu.VMEM((tq, hk), BF16),
                        pltpu.VMEM((tq, hk), BF16)],
        compiler_params=_params("arbitrary", "arbitrary"),
        name="band_prompt",
    )(h, g, wq, rrow, wout, kpad, vpad)


def _band_sample_kernel(q_ref, ck_ref, cv_ref, kn_ref, vn_ref, rrow_ref, *out_refs,
                        heads, write_cache):
    if write_cache:
        o_ref, cko_ref, cvo_ref, bias_ref = out_refs
    else:
        o_ref, bias_ref = out_refs
    nq = q_ref.shape[1]
    npast = ck_ref.shape[1]
    nk = npast + nq

    @pl.when(pl.program_id(0) == 0)
    def _():
        for hd in range(heads):
            bias_ref[hd] = _toeplitz_bias(rrow_ref[hd:hd + 1, :], nq, nk)

    k_all = jnp.concatenate([ck_ref[0], kn_ref[0]], axis=0)
    v_all = jnp.concatenate([cv_ref[0], vn_ref[0]], axis=0)
    if write_cache:
        cko_ref[0] = k_all[nq:]
        cvo_ref[0] = v_all[nq:]
    k_bf, v_bf = k_all.astype(BF16), v_all.astype(BF16)
    q = q_ref[0]
    outs = []
    for p in range(heads // 2):
        cols = slice(p * LANES, (p + 1) * LANES)
        outs.append(_attend_pair(q[:, cols], k_bf[:, cols], v_bf[:, cols],
                                 bias_ref[2 * p], bias_ref[2 * p + 1], None))
    o_ref[0] = jnp.concatenate(outs, axis=1).astype(BF16)


def _band_sample(q, ck, cv, kn, vn, rrow, *, heads, write_cache):
    bsz, nq, hk = q.shape
    npast = ck.shape[1]
    assert 2 * (hk // heads) == LANES and npast + nq <= ROLL_WIDTH - nq
    seq = lambda n: pl.BlockSpec((1, n, hk), lambda b: (b, 0, 0))
    out_specs = [seq(nq)]
    out_shape = [jax.ShapeDtypeStruct((bsz, nq, hk), BF16)]
    if write_cache:
        out_specs += [seq(npast), seq(npast)]
        out_shape += [jax.ShapeDtypeStruct((bsz, npast, hk), F32)] * 2
    return pl.pallas_call(
        functools.partial(_band_sample_kernel, heads=heads, write_cache=write_cache),
        grid=(bsz,),
        in_specs=[seq(nq), seq(npast), seq(npast), seq(nq), seq(nq), _resident(rrow.shape)],
        out_specs=out_specs,
        out_shape=out_shape,
        scratch_shapes=[pltpu.VMEM((heads, nq, npast + nq), F32)],
        compiler_params=_params("arbitrary"),
        name="band_sample",
    )(q, ck, cv, kn, vn, rrow)


def _bias_rows(table):
    clip = (table.shape[-1] - 1) // 2
    far = jnp.broadcast_to(table[..., -1:], table.shape[:-1] + (BAND_PAST - clip + 1,))
    near = jnp.flip(table[..., clip - CHUNK:-1], axis=-1)
    tail = jnp.broadcast_to(table[..., -1:], table.shape[:-1] + (ROLL_WIDTH - BAND - 1,))
    return jnp.concatenate([far, near, tail], axis=-1)


def kernel(x_prompt, x_sample, state_gla, cache_k, cache_v, norm_mix, norm_ffn, norm_final,
           gla_w_in, gla_w_gate2, gla_b_gate, gla_norm, gla_w_out, kv_norm, w_kv,
           attn_w_q, attn_rel_bias, attn_w_out, ffn_w_gate, ffn_w_up, ffn_w_down):
    bsz, seq, d = x_prompt.shape
    dbsz, dseq, _ = x_sample.shape
    n_a, _, heads_g, dk, dv = state_gla.shape
    depth = norm_mix.shape[0]
    _, npast, heads_a, dh = cache_k.shape
    hk = heads_a * dh
    nk, nv = heads_g * dk, heads_g * dv
    rank = gla_w_gate2.shape[1]
    assert seq % GLA_BLOCK == 0 and dseq % GLA_BLOCK != 0 and npast == BAND_PAST

    w_in = jnp.pad(gla_w_in, ((0, 0), (0, 0), (0, LANES - rank))).astype(BF16)
    w_g2 = jnp.pad(gla_w_gate2, ((0, 0), (0, LANES - rank), (0, 0))).astype(BF16)
    w_go = gla_w_out.astype(BF16)
    w_kvb = w_kv.astype(BF16)
    w_q = attn_w_q.astype(BF16)
    w_ao = attn_w_out.astype(BF16)
    w_fg, w_fu, w_fd = ffn_w_gate.astype(BF16), ffn_w_up.astype(BF16), ffn_w_down.astype(BF16)
    rrows = _bias_rows(attn_rel_bias)
    row = lambda a: a.reshape(1, -1)

    def ffn(h2, l):
        return _ffn(h2, row(norm_ffn[l]), w_fg[l], w_fu[l], w_fd[l], row(norm_final),
                    final=(l == depth - 1))

    h = x_prompt
    states = []
    for l in range(depth):
        if l < n_a:
            h, s = _gla_layer(h, (bsz, heads_g, dk, dv), row(norm_mix[l]), w_in[l], w_g2[l],
                              row(gla_b_gate[l]), row(gla_norm[l]), w_go[l],
                              blk=GLA_BLOCK, tt=512, chained=True)
            states.append(s)
        else:
            if l == n_a:
                kpad, vpad, k_last, v_last = _kv_prompt(h, row(kv_norm), w_kvb)
            i = l - n_a
            h = _band_prompt(h, kpad, vpad, row(norm_mix[l]), w_q[i], rrows[i], w_ao[i],
                             heads=heads_a)
        h = ffn(h.reshape(bsz * seq, d), l).reshape(bsz, seq, d)
    y_prompt = h
    state_prompt = jnp.stack(states)
    ck_prompt = k_last.reshape(bsz, -1, heads_a, dh)
    cv_prompt = v_last.reshape(bsz, -1, heads_a, dh)

    n_s = dbsz * dseq
    h = x_sample.reshape(1, n_s, d)
    ck = cache_k.reshape(dbsz, npast, hk)
    cv = cache_v.reshape(dbsz, npast, hk)
    states = []
    for l in range(depth):
        if l < n_a:
            h, s = _gla_layer(h, state_gla[l], row(norm_mix[l]), w_in[l], w_g2[l],
                              row(gla_b_gate[l]), row(gla_norm[l]), w_go[l],
                              blk=dseq, tt=128, chained=False)
            states.append(s)
            h2 = h.reshape(n_s, d)
        else:
            if l == n_a:
                kv_new = _norm_proj(h2, row(kv_norm), w_kvb)
                kn = kv_new[:, :hk].reshape(dbsz, dseq, hk)
                vn = kv_new[:, hk:].reshape(dbsz, dseq, hk)
            i = l - n_a
            q = _norm_proj(h2, row(norm_mix[l]), w_q[i], scale=dh ** -0.5, out_dtype=BF16)
            res = _band_sample(q.reshape(dbsz, dseq, hk), ck, cv, kn, vn, rrows[i],
                               heads=heads_a, write_cache=(l == n_a))
            if l == n_a:
                o, ck_new, cv_new = res
            else:
                o, = res
            h2 = _proj_res(o.reshape(n_s, hk), w_ao[i], h2)
        h2 = ffn(h2, l)
        h = h2.reshape(1, n_s, d)
    y_sample = h2.reshape(dbsz, dseq, d)
    state_sample = jnp.stack(states)
    ck_sample = ck_new.reshape(dbsz, npast, heads_a, dh)
    cv_sample = cv_new.reshape(dbsz, npast, heads_a, dh)

    return (y_prompt, y_sample, state_prompt, ck_prompt, cv_prompt,
            state_sample, ck_sample, cv_sample)
```

```python
import functools

import jax
import jax.numpy as jnp
from jax import lax
from jax.experimental import pallas as pl
from jax.experimental.pallas import tpu as pltpu

F32 = jnp.float32
BF16 = jnp.bfloat16

EPS = 1e-6
CHUNK = 64
N_PAST_CHUNKS = 8
BAND_PAST = N_PAST_CHUNKS * CHUNK
BAND = BAND_PAST + CHUNK
NEG_INF = -1e30
GLA_TAU = 16.0
GLA_BLOCK = 64
LANES = 128
ROLL_WIDTH = 640
VMEM_LIMIT_BYTES = 56 * 1024 * 1024


def _params(*sem):
    return pltpu.CompilerParams(dimension_semantics=sem, vmem_limit_bytes=VMEM_LIMIT_BYTES)


def _resident(shape):
    nd = len(shape)
    return pl.BlockSpec(shape, lambda *_: (0,) * nd, pipeline_mode=pl.Buffered(1))


def _rms(x, g):
    ms = jnp.mean(x * x, axis=-1, keepdims=True)
    return x * lax.rsqrt(ms + EPS) * g


def _dot(a, b):
    return jnp.dot(a, b, preferred_element_type=F32)


def _dot_nt(a, b):
    return lax.dot_general(a, b, (((1,), (1,)), ((), ())), preferred_element_type=F32)


def _dot_tn(a, b):
    return lax.dot_general(a, b, (((0,), (0,)), ((), ())), preferred_element_type=F32)


def _split3(x):
    hi = x.astype(BF16)
    r1 = x - hi.astype(F32)
    mid = r1.astype(BF16)
    lo = (r1 - mid.astype(F32)).astype(BF16)
    return hi, mid, lo


def _log_sigmoid(x):
    return jnp.minimum(x, 0.0) - jnp.log(1.0 + jnp.exp(-jnp.abs(x)))


def _ffn_kernel(h_ref, g_ref, wg_ref, wu_ref, wd_ref, gf_ref, o_ref, hh_ref, *, tf, final):
    h = h_ref[...]
    xn = _rms(h, g_ref[...]).astype(BF16)
    n_ff = wg_ref.shape[1]
    for c in range(n_ff // tf):
        sl = slice(c * tf, (c + 1) * tf)
        a = _dot(xn, wg_ref[:, sl])
        u = _dot(xn, wu_ref[:, sl])
        hh_ref[:, sl] = (a * jax.nn.sigmoid(a) * u).astype(BF16)
    y = h + _dot(hh_ref[...], wd_ref[...])
    if final:
        y = _rms(y, gf_ref[...])
    o_ref[...] = y


def _ffn(h, g, wg, wu, wd, gf, *, final, tm=512, tf=256):
    n, d = h.shape
    n_ff = wg.shape[1]
    tm = min(tm, n)
    assert n % tm == 0 and n_ff % tf == 0
    return pl.pallas_call(
        functools.partial(_ffn_kernel, tf=tf, final=final),
        grid=(n // tm,),
        in_specs=[
            pl.BlockSpec((tm, d), lambda i: (i, 0)),
            _resident((1, d)),
            _resident((d, n_ff)),
            _resident((d, n_ff)),
            _resident((n_ff, d)),
            _resident((1, d)),
        ],
        out_specs=pl.BlockSpec((tm, d), lambda i: (i, 0)),
        out_shape=jax.ShapeDtypeStruct((n, d), F32),
        scratch_shapes=[pltpu.VMEM((tm, n_ff), BF16)],
        compiler_params=_params("arbitrary"),
        name="ffn",
    )(h, g, wg, wu, wd, gf)


def _gla_kernel(*refs, blk, heads, dk, dv, chained):
    if chained:
        (h_ref, gm_ref, win_ref, wg2_ref, bg_ref, gn_ref, wout_ref,
         o_ref, s_ref, oacc_ref, upd_ref) = refs
        s_in_ref = s_ref
    else:
        (h_ref, gm_ref, win_ref, wg2_ref, bg_ref, gn_ref, wout_ref, s_in_ref,
         o_ref, s_ref, oacc_ref, upd_ref) = refs
    h = h_ref[0]
    tt = h.shape[0]
    nk, nv = heads * dk, heads * dv
    nblk = tt // blk

    if chained:
        @pl.when(pl.program_id(1) == 0)
        def _():
            s_ref[...] = jnp.zeros_like(s_ref)

    xn = _rms(h, gm_ref[...]).astype(BF16)
    z = _dot(xn, win_ref[...])
    q = z[:, :nk] * (dk ** -0.5)
    k = z[:, nk:2 * nk]
    v = z[:, 2 * nk:2 * nk + nv].astype(BF16)
    g = z[:, 2 * nk + nv:2 * nk + 2 * nv]
    a = z[:, 2 * nk + 2 * nv:].astype(BF16)
    logit = _dot(a, wg2_ref[...]) + bg_ref[...]
    la = _log_sigmoid(logit) * (1.0 / GLA_TAU)

    pos = lax.broadcasted_iota(jnp.int32, la.shape, 0) & (blk - 1)
    b = la
    step = 1
    while step < blk:
        b = b + jnp.where(pos >= step, pltpu.roll(b, step, 0), 0.0)
        step *= 2
    bt = jnp.concatenate(
        [jnp.broadcast_to(b[(j + 1) * blk - 1:(j + 1) * blk], (blk, nk)) for j in range(nblk)], axis=0)

    qd = (q * jnp.exp(b)).astype(BF16)
    kd = (k * jnp.exp(-b)).astype(BF16)
    kdec = (k * jnp.exp(bt - b)).astype(BF16)
    eb = jnp.exp(bt)

    tril = (lax.broadcasted_iota(jnp.int32, (blk, blk), 1)
            <= lax.broadcasted_iota(jnp.int32, (blk, blk), 0))
    items = [(j, hd) for j in range(nblk) for hd in range(heads)]
    rows = lambda j: slice(j * blk, (j + 1) * blk)
    kc = lambda hd: slice(hd * dk, (hd + 1) * dk)
    vc = lambda hd: slice(hd * dv, (hd + 1) * dv)

    att = {(j, hd): jnp.where(tril, _dot_nt(qd[rows(j), kc(hd)], kd[rows(j), kc(hd)]), 0.0)
           .astype(BF16) for j, hd in items}
    for j, hd in items:
        oacc_ref[rows(j), vc(hd)] = _dot(att[j, hd], v[rows(j), vc(hd)])
    for j, hd in items:
        upd_ref[j, hd] = _dot_tn(kdec[rows(j), kc(hd)], v[rows(j), vc(hd)])

    s = [s_in_ref[0, hd] for hd in range(heads)] if chained else None
    for j, hd in items:
        s_j = s[hd] if chained else s_in_ref[j, hd]
        oacc_ref[rows(j), vc(hd)] += _dot(qd[rows(j), kc(hd)], s_j.astype(BF16))
        eb_col = jnp.transpose(jnp.broadcast_to(eb[j * blk:j * blk + 1, kc(hd)], (dk, dk)))
        s_new = s_j * jnp.concatenate([eb_col] * (dv // dk), axis=1) + upd_ref[j, hd]
        if chained:
            s[hd] = s_new
        else:
            s_ref[j, hd] = s_new
    if chained:
        for hd in range(heads):
            s_ref[0, hd] = s[hd]

    o = oacc_ref[...]
    gn = gn_ref[...]
    ys = []
    for hd in range(heads):
        vc = slice(hd * dv, (hd + 1) * dv)
        oh = o[:, vc]
        ms = jnp.mean(oh * oh, axis=-1, keepdims=True)
        ys.append(oh * lax.rsqrt(ms + EPS) * gn[:, vc])
    y = jnp.concatenate(ys, axis=1)
    y = (y * (g * jax.nn.sigmoid(g))).astype(BF16)
    o_ref[0] = h + _dot(y, wout_ref[...])


def _gla_layer(h, s0, gm, win, wg2, bg, gn, wout, *, blk, tt, chained):
    bsz, t, d = h.shape
    heads, dk, dv = s0[1:] if chained else s0.shape[1:]
    nblk = tt // blk
    assert t % tt == 0 and tt % blk == 0
    weights = [gm, win, wg2, bg, gn, wout]
    in_specs = [pl.BlockSpec((1, tt, d), lambda b, i: (b, i, 0))]
    in_specs += [_resident(w.shape) for w in weights]
    args = [h] + weights
    if chained:
        nseq = bsz
        s_spec = pl.BlockSpec((1, heads, dk, dv), lambda b, i: (b, 0, 0, 0))
    else:
        nseq = bsz * (t // blk)
        s_spec = pl.BlockSpec((nblk, heads, dk, dv), lambda b, i: (b * (t // tt) + i, 0, 0, 0))
        in_specs.append(s_spec)
        args.append(s0)
    return pl.pallas_call(
        functools.partial(_gla_kernel, blk=blk, heads=heads, dk=dk, dv=dv, chained=chained),
        grid=(bsz, t // tt),
        in_specs=in_specs,
        out_specs=[pl.BlockSpec((1, tt, d), lambda b, i: (b, i, 0)), s_spec],
        out_shape=[jax.ShapeDtypeStruct((bsz, t, d), F32),
                   jax.ShapeDtypeStruct((nseq, heads, dk, dv), F32)],
        scratch_shapes=[pltpu.VMEM((tt, heads * dv), F32),
                        pltpu.VMEM((nblk, heads, dk, dv), F32)],
        compiler_params=_params("arbitrary", "arbitrary"),
        name="gla_layer",
    )(*args)


def _norm_proj_kernel(h_ref, g_ref, w_ref, o_ref, *, scale):
    xn = _rms(h_ref[...], g_ref[...]).astype(BF16)
    y = _dot(xn, w_ref[...])
    if scale != 1.0:
        y = y * scale
    o_ref[...] = y.astype(o_ref.dtype)


def _norm_proj(h, g, w, *, scale=1.0, out_dtype=F32, tm=512):
    n, d = h.shape
    tm = min(tm, n)
    assert n % tm == 0
    return pl.pallas_call(
        functools.partial(_norm_proj_kernel, scale=scale),
        grid=(n // tm,),
        in_specs=[pl.BlockSpec((tm, d), lambda i: (i, 0)), _resident((1, d)), _resident(w.shape)],
        out_specs=pl.BlockSpec((tm, w.shape[1]), lambda i: (i, 0)),
        out_shape=jax.ShapeDtypeStruct((n, w.shape[1]), out_dtype),
        compiler_params=_params("arbitrary"),
        name="norm_proj",
    )(h, g, w)


def _proj_res_kernel(x_ref, w_ref, h_ref, o_ref):
    o_ref[...] = h_ref[...] + _dot(x_ref[...], w_ref[...])


def _proj_res(x, w, h, *, tm=512):
    n, d = h.shape
    tm = min(tm, n)
    assert n % tm == 0
    return pl.pallas_call(
        _proj_res_kernel,
        grid=(n // tm,),
        in_specs=[pl.BlockSpec((tm, x.shape[1]), lambda i: (i, 0)), _resident(w.shape),
                  pl.BlockSpec((tm, d), lambda i: (i, 0))],
        out_specs=pl.BlockSpec((tm, d), lambda i: (i, 0)),
        out_shape=jax.ShapeDtypeStruct((n, d), F32),
        compiler_params=_params("arbitrary"),
        name="proj_res",
    )(x, w, h)


def _kv_prompt_kernel(h_ref, g_ref, w_ref, kp_ref, vp_ref, kl_ref, vl_ref):
    j = pl.program_id(1)
    hk = kp_ref.shape[2]

    @pl.when(j == 0)
    def _():
        kp_ref[...] = jnp.zeros_like(kp_ref)
        vp_ref[...] = jnp.zeros_like(vp_ref)

    @pl.when(j > 0)
    def _():
        xn = _rms(h_ref[0], g_ref[...]).astype(BF16)
        z = _dot(xn, w_ref[...])
        kp_ref[0] = z[:, :hk].astype(BF16)
        vp_ref[0] = z[:, hk:].astype(BF16)

        @pl.when(j == pl.num_programs(1) - 1)
        def _():
            kl_ref[0] = z[:, :hk]
            vl_ref[0] = z[:, hk:]


def _kv_prompt(h, g, w):
    bsz, t, d = h.shape
    hk = w.shape[1] // 2
    tt = BAND_PAST
    assert t % tt == 0
    nt = t // tt
    pad_spec = pl.BlockSpec((1, tt, hk), lambda b, j: (b, j, 0))
    last_spec = pl.BlockSpec((1, tt, hk), lambda b, j: (b, 0, 0))
    return pl.pallas_call(
        _kv_prompt_kernel,
        grid=(bsz, nt + 1),
        in_specs=[pl.BlockSpec((1, tt, d), lambda b, j: (b, jnp.maximum(j - 1, 0), 0)),
                  _resident((1, d)), _resident(w.shape)],
        out_specs=[pad_spec, pad_spec, last_spec, last_spec],
        out_shape=[jax.ShapeDtypeStruct((bsz, t + tt, hk), BF16)] * 2
                  + [jax.ShapeDtypeStruct((bsz, tt, hk), F32)] * 2,
        compiler_params=_params("arbitrary", "arbitrary"),
        name="kv_prompt",
    )(h, g, w)


def _toeplitz_bias(row, nq, nk):
    full = pltpu.roll(jnp.broadcast_to(row, (nq, ROLL_WIDTH)), 0, 1, stride=1, stride_axis=0)
    return full[:, :nk]


def _attend_pair(qp, kw, vw, bias_a, bias_b, thr):
    nq, nk = qp.shape[0], kw.shape[0]
    lane = lax.broadcasted_iota(jnp.int32, (nq, LANES), 1)
    first = lane < (LANES // 2)
    out = None
    for sel, bias in ((first, bias_a), (~first, bias_b)):
        qm = jnp.where(sel, qp, jnp.zeros_like(qp))
        s = _dot_nt(qm, kw) + bias
        if thr is not None:
            col = lax.broadcasted_iota(jnp.int32, (nq, nk), 1)
            s = jnp.where(col >= thr, s, NEG_INF)
        m = jnp.max(s, axis=-1, keepdims=True)
        e = jnp.exp(s - m)
        l = jnp.sum(e, axis=-1, keepdims=True)
        pv = _dot(e.astype(BF16), vw) / l
        out = pv if out is None else jnp.where(first, out, pv)
    return out


def _group_bias(row, grp):
    nq, win = grp * CHUNK, (grp + N_PAST_CHUNKS) * CHUNK
    full = pltpu.roll(jnp.broadcast_to(row, (nq, ROLL_WIDTH)), 0, 1, stride=1, stride_axis=0)
    if win > ROLL_WIDTH:
        full = jnp.concatenate([full, full[:, :win - ROLL_WIDTH]], axis=1)
    r = lax.broadcasted_iota(jnp.int32, (nq, win), 0)
    c = lax.broadcasted_iota(jnp.int32, (nq, win), 1)
    off = c - (r & -CHUNK)
    return jnp.where((off >= 0) & (off < BAND), full[:, :win], NEG_INF)


def _band_prompt_kernel(h_ref, g_ref, wq_ref, rrow_ref, wout_ref, k_ref, v_ref, o_ref,
                        bias_ref, q_ref, oacc_ref, *, heads, grp):
    t = pl.program_id(1)
    tq = h_ref.shape[1]
    nq, win = grp * CHUNK, (grp + N_PAST_CHUNKS) * CHUNK
    half = LANES // 2

    @pl.when((pl.program_id(0) == 0) & (t == 0))
    def _():
        for hd in range(heads):
            bias_ref[hd // 2, (hd % 2) * nq:(hd % 2 + 1) * nq, :] = _group_bias(
                rrow_ref[hd:hd + 1, :], grp)

    h = h_ref[0]
    xn = _rms(h, g_ref[...]).astype(BF16)
    dh = wq_ref.shape[1] // heads
    q_ref[...] = (_dot(xn, wq_ref[...]) * (dh ** -0.5)).astype(BF16)

    def run(masked):
        def group_body(i, carry):
            r0 = pl.multiple_of(i * nq, nq)
            first = lax.broadcasted_iota(jnp.int32, (nq, LANES), 1) < half
            pairs = range(heads // 2)
            cols = [slice(p * LANES, (p + 1) * LANES) for p in pairs]
            scores = []
            for p in pairs:
                qp = q_ref[pl.ds(r0, nq), cols[p]]
                zero = jnp.zeros_like(qp)
                q2 = jnp.concatenate([jnp.where(first, qp, zero), jnp.where(first, zero, qp)],
                                     axis=0)
                scores.append(_dot_nt(q2, k_ref[0, pl.ds(r0, win), cols[p]]))
            probs, norms = [], []
            for p in pairs:
                s = scores[p] + bias_ref[p]
                if masked:
                    key = lax.broadcasted_iota(jnp.int32, s.shape, 1)
                    s = jnp.where(key >= BAND_PAST - i * nq, s, NEG_INF)
                e = jnp.exp(s - jnp.max(s, axis=-1, keepdims=True))
                norms.append(jnp.sum(e, axis=-1, keepdims=True))
                probs.append(e.astype(BF16))
            for p in pairs:
                o2 = _dot(probs[p], v_ref[0, pl.ds(r0, win), cols[p]]) / norms[p]
                oacc_ref[pl.ds(r0, nq), cols[p]] = jnp.where(first, o2[:nq], o2[nq:]).astype(BF16)
            return carry

        lax.fori_loop(0, tq // nq, group_body, 0)

    @pl.when(t == 0)
    def _():
        run(True)

    @pl.when(t != 0)
    def _():
        run(False)

    o_ref[0] = h + _dot(oacc_ref[...], wout_ref[...])


def _band_prompt(h, kpad, vpad, g, wq, rrow, wout, *, heads, tq=512, grp=2):
    bsz, t, d = h.shape
    hk = wq.shape[1]
    nq, win = grp * CHUNK, (grp + N_PAST_CHUNKS) * CHUNK
    assert t % tq == 0 and tq % nq == 0 and 2 * (hk // heads) == LANES
    assert ROLL_WIDTH <= win < 2 * ROLL_WIDTH
    tile_win = tq + BAND_PAST
    win_spec = pl.BlockSpec((pl.Element(1), pl.Element(tile_win), pl.Element(hk)),
                            lambda b, i: (b, i * tq, 0))
    return pl.pallas_call(
        functools.partial(_band_prompt_kernel, heads=heads, grp=grp),
        grid=(bsz, t // tq),
        in_specs=[pl.BlockSpec((1, tq, d), lambda b, i: (b, i, 0)),
                  _resident((1, d)), _resident(wq.shape), _resident(rrow.shape),
                  _resident(wout.shape), win_spec, win_spec],
        out_specs=pl.BlockSpec((1, tq, d), lambda b, i: (b, i, 0)),
        out_shape=jax.ShapeDtypeStruct((bsz, t, d), F32),
        scratch_shapes=[pltpu.VMEM((heads // 2, 2 * nq, win), F32),
                        pltpu.VMEM((tq, hk), BF16),
                        pltpu.VMEM((tq, hk), BF16)],
        compiler_params=_params("arbitrary", "arbitrary"),
        name="band_prompt",
    )(h, g, wq, rrow, wout, kpad, vpad)


def _band_sample_kernel(q_ref, ck_ref, cv_ref, kn_ref, vn_ref, rrow_ref, *out_refs,
                        heads, write_cache):
    if write_cache:
        o_ref, cko_ref, cvo_ref, bias_ref = out_refs
    else:
        o_ref, bias_ref = out_refs
    nq = q_ref.shape[1]
    npast = ck_ref.shape[1]
    nk = npast + nq

    @pl.when(pl.program_id(0) == 0)
    def _():
        for hd in range(heads):
            bias_ref[hd] = _toeplitz_bias(rrow_ref[hd:hd + 1, :], nq, nk)

    k_all = jnp.concatenate([ck_ref[0], kn_ref[0]], axis=0)
    v_all = jnp.concatenate([cv_ref[0], vn_ref[0]], axis=0)
    if write_cache:
        cko_ref[0] = k_all[nq:]
        cvo_ref[0] = v_all[nq:]
    k_bf, v_bf = k_all.astype(BF16), v_all.astype(BF16)
    q = q_ref[0]
    outs = []
    for p in range(heads // 2):
        cols = slice(p * LANES, (p + 1) * LANES)
        outs.append(_attend_pair(q[:, cols], k_bf[:, cols], v_bf[:, cols],
                                 bias_ref[2 * p], bias_ref[2 * p + 1], None))
    o_ref[0] = jnp.concatenate(outs, axis=1).astype(BF16)


def _band_sample(q, ck, cv, kn, vn, rrow, *, heads, write_cache):
    bsz, nq, hk = q.shape
    npast = ck.shape[1]
    assert 2 * (hk // heads) == LANES and npast + nq <= ROLL_WIDTH - nq
    seq = lambda n: pl.BlockSpec((1, n, hk), lambda b: (b, 0, 0))
    out_specs = [seq(nq)]
    out_shape = [jax.ShapeDtypeStruct((bsz, nq, hk), BF16)]
    if write_cache:
        out_specs += [seq(npast), seq(npast)]
        out_shape += [jax.ShapeDtypeStruct((bsz, npast, hk), F32)] * 2
    return pl.pallas_call(
        functools.partial(_band_sample_kernel, heads=heads, write_cache=write_cache),
        grid=(bsz,),
        in_specs=[seq(nq), seq(npast), seq(npast), seq(nq), seq(nq), _resident(rrow.shape)],
        out_specs=out_specs,
        out_shape=out_shape,
        scratch_shapes=[pltpu.VMEM((heads, nq, npast + nq), F32)],
        compiler_params=_params("arbitrary"),
        name="band_sample",
    )(q, ck, cv, kn, vn, rrow)


def _bias_rows(table):
    clip = (table.shape[-1] - 1) // 2
    far = jnp.broadcast_to(table[..., -1:], table.shape[:-1] + (BAND_PAST - clip + 1,))
    near = jnp.flip(table[..., clip - CHUNK:-1], axis=-1)
    tail = jnp.broadcast_to(table[..., -1:], table.shape[:-1] + (ROLL_WIDTH - BAND - 1,))
    return jnp.concatenate([far, near, tail], axis=-1)


def kernel(x_prompt, x_sample, state_gla, cache_k, cache_v, norm_mix, norm_ffn, norm_final,
           gla_w_in, gla_w_gate2, gla_b_gate, gla_norm, gla_w_out, kv_norm, w_kv,
           attn_w_q, attn_rel_bias, attn_w_out, ffn_w_gate, ffn_w_up, ffn_w_down):
    bsz, seq, d = x_prompt.shape
    dbsz, dseq, _ = x_sample.shape
    n_a, _, heads_g, dk, dv = state_gla.shape
    depth = norm_mix.shape[0]
    _, npast, heads_a, dh = cache_k.shape
    hk = heads_a * dh
    nk, nv = heads_g * dk, heads_g * dv
    rank = gla_w_gate2.shape[1]
    assert seq % GLA_BLOCK == 0 and dseq % GLA_BLOCK != 0 and npast == BAND_PAST

    w_in = jnp.pad(gla_w_in, ((0, 0), (0, 0), (0, LANES - rank))).astype(BF16)
    w_g2 = jnp.pad(gla_w_gate2, ((0, 0), (0, LANES - rank), (0, 0))).astype(BF16)
    w_go = gla_w_out.astype(BF16)
    w_kvb = w_kv.astype(BF16)
    w_q = attn_w_q.astype(BF16)
    w_ao = attn_w_out.astype(BF16)
    w_fg, w_fu, w_fd = ffn_w_gate.astype(BF16), ffn_w_up.astype(BF16), ffn_w_down.astype(BF16)
    rrows = _bias_rows(attn_rel_bias)
    row = lambda a: a.reshape(1, -1)

    def ffn(h2, l):
        return _ffn(h2, row(norm_ffn[l]), w_fg[l], w_fu[l], w_fd[l], row(norm_final),
                    final=(l == depth - 1))

    h = x_prompt
    states = []
    for l in range(depth):
        if l < n_a:
            h, s = _gla_layer(h, (bsz, heads_g, dk, dv), row(norm_mix[l]), w_in[l], w_g2[l],
                              row(gla_b_gate[l]), row(gla_norm[l]), w_go[l],
                              blk=GLA_BLOCK, tt=512, chained=True)
            states.append(s)
        else:
            if l == n_a:
                kpad, vpad, k_last, v_last = _kv_prompt(h, row(kv_norm), w_kvb)
            i = l - n_a
            h = _band_prompt(h, kpad, vpad, row(norm_mix[l]), w_q[i], rrows[i], w_ao[i],
                             heads=heads_a)
        h = ffn(h.reshape(bsz * seq, d), l).reshape(bsz, seq, d)
    y_prompt = h
    state_prompt = jnp.stack(states)
    ck_prompt = k_last.reshape(bsz, -1, heads_a, dh)
    cv_prompt = v_last.reshape(bsz, -1, heads_a, dh)

    n_s = dbsz * dseq
    h = x_sample.reshape(1, n_s, d)
    ck = cache_k.reshape(dbsz, npast, hk)
    cv = cache_v.reshape(dbsz, npast, hk)
    states = []
    for l in range(depth):
        if l < n_a:
            h, s = _gla_layer(h, state_gla[l], row(norm_mix[l]), w_in[l], w_g2[l],
                              row(gla_b_gate[l]), row(gla_norm[l]), w_go[l],
                              blk=dseq, tt=128, chained=False)
            states.append(s)
            h2 = h.reshape(n_s, d)
        else:
            if l == n_a:
                kv_new = _norm_proj(h2, row(kv_norm), w_kvb)
                kn = kv_new[:, :hk].reshape(dbsz, dseq, hk)
                vn = kv_new[:, hk:].reshape(dbsz, dseq, hk)
            i = l - n_a
            q = _norm_proj(h2, row(norm_mix[l]), w_q[i], scale=dh ** -0.5, out_dtype=BF16)
            res = _band_sample(q.reshape(dbsz, dseq, hk), ck, cv, kn, vn, rrows[i],
                               heads=heads_a, write_cache=(l == n_a))
            if l == n_a:
                o, ck_new, cv_new = res
            else:
                o, = res
            h2 = _proj_res(o.reshape(n_s, hk), w_ao[i], h2)
        h2 = ffn(h2, l)
        h = h2.reshape(1, n_s, d)
    y_sample = h2.reshape(dbsz, dseq, d)
    state_sample = jnp.stack(states)
    ck_sample = ck_new.reshape(dbsz, npast, heads_a, dh)
    cv_sample = cv_new.reshape(dbsz, npast, heads_a, dh)

    return (y_prompt, y_sample, state_prompt, ck_prompt, cv_prompt,
            state_sample, ck_sample, cv_sample)
```

```python
import functools

import jax
import jax.numpy as jnp
from jax import lax
from jax.experimental import pallas as pl
from jax.experimental.pallas import tpu as pltpu

F32 = jnp.float32
BF16 = jnp.bfloat16

EPS = 1e-6
CHUNK = 64
N_PAST_CHUNKS = 8
BAND_PAST = N_PAST_CHUNKS * CHUNK
BAND = BAND_PAST + CHUNK
NEG_INF = -1e30
LOG2E = 1.4426950408889634
GLA_TAU = 16.0
GLA_BLOCK = 64
LANES = 128
ROLL_WIDTH = 640
VMEM_LIMIT_BYTES = 56 * 1024 * 1024


def _params(*sem):
    return pltpu.CompilerParams(dimension_semantics=sem, vmem_limit_bytes=VMEM_LIMIT_BYTES)


def _resident(shape):
    nd = len(shape)
    return pl.BlockSpec(shape, lambda *_: (0,) * nd, pipeline_mode=pl.Buffered(1))


def _rms(x, g):
    ms = jnp.mean(x * x, axis=-1, keepdims=True)
    return x * lax.rsqrt(ms + EPS) * g


def _dot(a, b):
    return jnp.dot(a, b, preferred_element_type=F32)


def _dot_nt(a, b):
    return lax.dot_general(a, b, (((1,), (1,)), ((), ())), preferred_element_type=F32)


def _dot_tn(a, b):
    return lax.dot_general(a, b, (((0,), (0,)), ((), ())), preferred_element_type=F32)


def _split3(x):
    hi = x.astype(BF16)
    r1 = x - hi.astype(F32)
    mid = r1.astype(BF16)
    lo = (r1 - mid.astype(F32)).astype(BF16)
    return hi, mid, lo


def _log_sigmoid(x):
    return jnp.minimum(x, 0.0) - jnp.log(1.0 + jnp.exp(-jnp.abs(x)))


def _ffn_kernel(h_ref, g_ref, wg_ref, wu_ref, wd_ref, gf_ref, o_ref, hh_ref, *, tf, final):
    h = h_ref[...]
    xn = _rms(h, g_ref[...]).astype(BF16)
    n_ff = wg_ref.shape[1]
    for c in range(n_ff // tf):
        sl = slice(c * tf, (c + 1) * tf)
        a = _dot(xn, wg_ref[:, sl])
        u = _dot(xn, wu_ref[:, sl])
        hh_ref[:, sl] = (a * jax.nn.sigmoid(a) * u).astype(BF16)
    y = h + _dot(hh_ref[...], wd_ref[...])
    if final:
        y = _rms(y, gf_ref[...])
    o_ref[...] = y


def _ffn(h, g, wg, wu, wd, gf, *, final, tm=512, tf=256):
    n, d = h.shape
    n_ff = wg.shape[1]
    tm = min(tm, n)
    assert n % tm == 0 and n_ff % tf == 0
    return pl.pallas_call(
        functools.partial(_ffn_kernel, tf=tf, final=final),
        grid=(n // tm,),
        in_specs=[
            pl.BlockSpec((tm, d), lambda i: (i, 0)),
            _resident((1, d)),
            _resident((d, n_ff)),
            _resident((d, n_ff)),
            _resident((n_ff, d)),
            _resident((1, d)),
        ],
        out_specs=pl.BlockSpec((tm, d), lambda i: (i, 0)),
        out_shape=jax.ShapeDtypeStruct((n, d), F32),
        scratch_shapes=[pltpu.VMEM((tm, n_ff), BF16)],
        compiler_params=_params("arbitrary"),
        name="ffn",
    )(h, g, wg, wu, wd, gf)


def _gla_kernel(*refs, blk, heads, dk, dv, chained, has_prev):
    h_ref, gm_ref, win_ref, wg2_ref, bg_ref, gn_ref, wout_ref = refs[:7]
    o_ref, s_ref, oacc_ref, upd_ref = refs[-4:]
    s_in_ref = s_ref if chained else refs[7]
    assert len(refs) == 11 + (not chained) + has_prev
    h = h_ref[0]
    tt = h.shape[0]
    nk, nv = heads * dk, heads * dv
    nblk = tt // blk

    if chained:
        @pl.when(pl.program_id(1) == 0)
        def _():
            s_ref[...] = jnp.zeros_like(s_ref)

    xn = _rms(h, gm_ref[...]).astype(BF16)
    z = _dot(xn, win_ref[...])
    q = z[:, :nk] * (dk ** -0.5)
    k = z[:, nk:2 * nk]
    v = z[:, 2 * nk:2 * nk + nv].astype(BF16)
    g = z[:, 2 * nk + nv:2 * nk + 2 * nv]
    a = z[:, 2 * nk + 2 * nv:].astype(BF16)
    logit = _dot(a, wg2_ref[...]) + bg_ref[...]
    la = _log_sigmoid(logit) * (1.0 / GLA_TAU)

    pos = lax.broadcasted_iota(jnp.int32, la.shape, 0) & (blk - 1)
    b = la
    step = 1
    while step < blk:
        b = b + jnp.where(pos >= step, pltpu.roll(b, step, 0), 0.0)
        step *= 2
    bt = jnp.concatenate(
        [jnp.broadcast_to(b[(j + 1) * blk - 1:(j + 1) * blk], (blk, nk)) for j in range(nblk)], axis=0)

    qd = (q * jnp.exp(b)).astype(BF16)
    kd = (k * jnp.exp(-b)).astype(BF16)
    kdec = (k * jnp.exp(bt - b)).astype(BF16)
    eb = jnp.exp(bt)

    tril = (lax.broadcasted_iota(jnp.int32, (blk, blk), 1)
            <= lax.broadcasted_iota(jnp.int32, (blk, blk), 0))
    items = [(j, hd) for j in range(nblk) for hd in range(heads)]
    rows = lambda j: slice(j * blk, (j + 1) * blk)
    kc = lambda hd: slice(hd * dk, (hd + 1) * dk)
    vc = lambda hd: slice(hd * dv, (hd + 1) * dv)

    att = {(j, hd): jnp.where(tril, _dot_nt(qd[rows(j), kc(hd)], kd[rows(j), kc(hd)]), 0.0)
           .astype(BF16) for j, hd in items}
    for j, hd in items:
        oacc_ref[rows(j), vc(hd)] = _dot(att[j, hd], v[rows(j), vc(hd)])
    for j, hd in items:
        upd_ref[j, hd] = _dot_tn(kdec[rows(j), kc(hd)], v[rows(j), vc(hd)])

    s = [s_in_ref[0, hd] for hd in range(heads)] if chained else None
    for j, hd in items:
        s_j = s[hd] if chained else s_in_ref[j, hd]
        oacc_ref[rows(j), vc(hd)] += _dot(qd[rows(j), kc(hd)], s_j.astype(BF16))
        eb_col = jnp.transpose(jnp.broadcast_to(eb[j * blk:j * blk + 1, kc(hd)], (dk, dk)))
        s_new = s_j * jnp.concatenate([eb_col] * (dv // dk), axis=1) + upd_ref[j, hd]
        if chained:
            s[hd] = s_new
        else:
            s_ref[j, hd] = s_new
    if chained:
        for hd in range(heads):
            s_ref[0, hd] = s[hd]

    o = oacc_ref[...]
    gn = gn_ref[...]
    ys = []
    for hd in range(heads):
        vc = slice(hd * dv, (hd + 1) * dv)
        oh = o[:, vc]
        ms = jnp.mean(oh * oh, axis=-1, keepdims=True)
        ys.append(oh * lax.rsqrt(ms + EPS) * gn[:, vc])
    y = jnp.concatenate(ys, axis=1)
    y = (y * (g * jax.nn.sigmoid(g))).astype(BF16)
    o_ref[0] = h + _dot(y, wout_ref[...])


def _gla_layer(h, s0, gm, win, wg2, bg, gn, wout, *, blk, tt, chained, layer, n_layers, prev):
    bsz, t, d = h.shape
    heads, dk, dv = s0[1:] if chained else s0.shape[1:]
    nblk = tt // blk
    assert t % tt == 0 and tt % blk == 0
    weights = [gm, win, wg2, bg, gn, wout]
    in_specs = [pl.BlockSpec((1, tt, d), lambda b, i: (b, i, 0))]
    in_specs += [_resident(w.shape) for w in weights]
    args = [h] + weights
    if chained:
        nseq = bsz
        s_out_spec = pl.BlockSpec((1, heads, dk, dv), lambda b, i: (layer * bsz + b, 0, 0, 0))
    else:
        nseq = bsz * (t // blk)
        steps = t // tt
        in_specs.append(pl.BlockSpec((nblk, heads, dk, dv),
                                     lambda b, i: (b * steps + i, 0, 0, 0)))
        args.append(s0)
        s_out_spec = pl.BlockSpec((nblk, heads, dk, dv),
                                  lambda b, i: (layer * bsz * steps + b * steps + i, 0, 0, 0))
    aliases = {}
    if prev is not None:
        aliases = {len(args): 1}
        in_specs.append(pl.BlockSpec(memory_space=pl.ANY))
        args.append(prev)
    return pl.pallas_call(
        functools.partial(_gla_kernel, blk=blk, heads=heads, dk=dk, dv=dv, chained=chained,
                          has_prev=prev is not None),
        grid=(bsz, t // tt),
        in_specs=in_specs,
        out_specs=[pl.BlockSpec((1, tt, d), lambda b, i: (b, i, 0)), s_out_spec],
        out_shape=[jax.ShapeDtypeStruct((bsz, t, d), F32),
                   jax.ShapeDtypeStruct((n_layers * nseq, heads, dk, dv), F32)],
        input_output_aliases=aliases,
        scratch_shapes=[pltpu.VMEM((tt, heads * dv), F32),
                        pltpu.VMEM((nblk, heads, dk, dv), F32)],
        compiler_params=_params("arbitrary", "arbitrary"),
        name="gla_layer",
    )(*args)


def _norm_proj_kernel(h_ref, g_ref, w_ref, o_ref, *, scale):
    xn = _rms(h_ref[...], g_ref[...]).astype(BF16)
    y = _dot(xn, w_ref[...])
    if scale != 1.0:
        y = y * scale
    o_ref[...] = y.astype(o_ref.dtype)


def _norm_proj(h, g, w, *, scale=1.0, out_dtype=F32, tm=512):
    n, d = h.shape
    tm = min(tm, n)
    assert n % tm == 0
    return pl.pallas_call(
        functools.partial(_norm_proj_kernel, scale=scale),
        grid=(n // tm,),
        in_specs=[pl.BlockSpec((tm, d), lambda i: (i, 0)), _resident((1, d)), _resident(w.shape)],
        out_specs=pl.BlockSpec((tm, w.shape[1]), lambda i: (i, 0)),
        out_shape=jax.ShapeDtypeStruct((n, w.shape[1]), out_dtype),
        compiler_params=_params("arbitrary"),
        name="norm_proj",
    )(h, g, w)


def _proj_res_kernel(x_ref, w_ref, h_ref, o_ref):
    o_ref[...] = h_ref[...] + _dot(x_ref[...], w_ref[...])


def _proj_res(x, w, h, *, tm=512):
    n, d = h.shape
    tm = min(tm, n)
    assert n % tm == 0
    return pl.pallas_call(
        _proj_res_kernel,
        grid=(n // tm,),
        in_specs=[pl.BlockSpec((tm, x.shape[1]), lambda i: (i, 0)), _resident(w.shape),
                  pl.BlockSpec((tm, d), lambda i: (i, 0))],
        out_specs=pl.BlockSpec((tm, d), lambda i: (i, 0)),
        out_shape=jax.ShapeDtypeStruct((n, d), F32),
        compiler_params=_params("arbitrary"),
        name="proj_res",
    )(x, w, h)


def _kv_prompt_kernel(h_ref, g_ref, w_ref, kp_ref, vt_ref, kl_ref, vl_ref):
    j = pl.program_id(1)
    hk = kp_ref.shape[2]

    @pl.when(j == 0)
    def _():
        kp_ref[...] = jnp.zeros_like(kp_ref)
        vt_ref[...] = jnp.zeros_like(vt_ref)

    @pl.when(j > 0)
    def _():
        xn = _rms(h_ref[0], g_ref[...]).astype(BF16)
        z = _dot(xn, w_ref[...])
        kp_ref[0] = z[:, :hk].astype(BF16)
        for jt in range(vt_ref.shape[1]):
            vt_ref[0, jt] = jnp.transpose(z[jt * LANES:(jt + 1) * LANES, hk:]).astype(BF16)

        @pl.when(j == pl.num_programs(1) - 1)
        def _():
            kl_ref[0] = z[:, :hk]
            vl_ref[0] = z[:, hk:]


def _kv_prompt(h, g, w):
    bsz, t, d = h.shape
    hk = w.shape[1] // 2
    tt = BAND_PAST
    assert t % tt == 0 and tt % LANES == 0
    nt = t // tt
    last_spec = pl.BlockSpec((1, tt, hk), lambda b, j: (b, 0, 0))
    return pl.pallas_call(
        _kv_prompt_kernel,
        grid=(bsz, nt + 1),
        in_specs=[pl.BlockSpec((1, tt, d), lambda b, j: (b, jnp.maximum(j - 1, 0), 0)),
                  _resident((1, d)), _resident(w.shape)],
        out_specs=[pl.BlockSpec((1, tt, hk), lambda b, j: (b, j, 0)),
                   pl.BlockSpec((1, tt // LANES, hk, LANES), lambda b, j: (b, j, 0, 0)),
                   last_spec, last_spec],
        out_shape=[jax.ShapeDtypeStruct((bsz, t + tt, hk), BF16),
                   jax.ShapeDtypeStruct((bsz, (t + tt) // LANES, hk, LANES), BF16)]
                  + [jax.ShapeDtypeStruct((bsz, tt, hk), F32)] * 2,
        compiler_params=_params("arbitrary", "arbitrary"),
        name="kv_prompt",
    )(h, g, w)


def _toeplitz_bias(row, nq, nk):
    full = pltpu.roll(jnp.broadcast_to(row, (nq, ROLL_WIDTH)), 0, 1, stride=1, stride_axis=0)
    return full[:, :nk]


def _group_bias(row, grp):
    nq, win = grp * CHUNK, (grp + N_PAST_CHUNKS) * CHUNK
    full = pltpu.roll(jnp.broadcast_to(row, (nq, ROLL_WIDTH)), 0, 1, stride=1, stride_axis=0)
    if win > ROLL_WIDTH:
        full = jnp.concatenate([full, full[:, :win - ROLL_WIDTH]], axis=1)
    r = lax.broadcasted_iota(jnp.int32, (nq, win), 0)
    c = lax.broadcasted_iota(jnp.int32, (nq, win), 1)
    off = c - (r & -CHUNK)
    return jnp.where((off >= 0) & (off < BAND), full[:, :win], NEG_INF)


def _band_prompt_kernel(h_ref, g_ref, wq_ref, rrow_ref, wout_ref, k_ref, vt_ref, o_ref,
                        bias_ref, q_ref, oacc_ref, *, heads, grp):
    t = pl.program_id(1)
    tq = h_ref.shape[1]
    nq, win = grp * CHUNK, (grp + N_PAST_CHUNKS) * CHUNK
    half = LANES // 2
    pairs = range(heads // 2)
    cols = [slice(p * LANES, (p + 1) * LANES) for p in pairs]

    @pl.when((pl.program_id(0) == 0) & (t == 0))
    def _():
        for p in pairs:
            bias_ref[p] = jnp.concatenate(
                [jnp.transpose(_group_bias(rrow_ref[hd:hd + 1, :] * LOG2E, grp))
                 for hd in (2 * p, 2 * p + 1)], axis=1)

    h = h_ref[0]
    xn = _rms(h, g_ref[...]).astype(BF16)
    dh = wq_ref.shape[1] // heads
    q_ref[...] = (_dot(xn, wq_ref[...]) * (dh ** -0.5 * LOG2E)).astype(BF16)

    first = lax.broadcasted_iota(jnp.int32, (nq, LANES), 1) < half

    def group_body(gi, masked):
        r0 = pl.multiple_of(gi * nq, nq)
        scores = []
        for p in pairs:
            qp = q_ref[pl.ds(r0, nq), cols[p]]
            zero = jnp.zeros_like(qp)
            q2 = jnp.concatenate([jnp.where(first, qp, zero), jnp.where(first, zero, qp)], axis=0)
            scores.append(_dot_nt(k_ref[0, pl.ds(r0, win), cols[p]], q2))
        probs, invs = [], []
        for p in pairs:
            s = scores[p] + bias_ref[p]
            if masked:
                key = lax.broadcasted_iota(jnp.int32, s.shape, 0)
                s = jnp.where(key >= BAND_PAST - gi * nq, s, NEG_INF)
            e = jnp.exp2(s - jnp.max(s, axis=0, keepdims=True))
            invs.append(1.0 / jnp.sum(e, axis=0, keepdims=True))
            probs.append(e.astype(BF16))
        for p in pairs:
            tiles = vt_ref[0, pl.ds(gi * (nq // LANES), win // LANES), cols[p], :]
            vt = jnp.concatenate([tiles[j] for j in range(win // LANES)], axis=1)
            o_t = _dot(vt, probs[p]) * invs[p]
            o_t = jnp.concatenate([o_t[:half, :nq], o_t[half:, nq:]], axis=0)
            oacc_ref[pl.ds(r0, nq), cols[p]] = jnp.transpose(o_t).astype(BF16)

    def run(masked):
        def body(gi, carry):
            group_body(gi, masked)
            return carry

        lax.fori_loop(0, tq // nq, body, 0)

    @pl.when(t == 0)
    def _():
        run(True)

    @pl.when(t != 0)
    def _():
        run(False)

    o_ref[0] = h + _dot(oacc_ref[...], wout_ref[...])


def _band_prompt(h, kpad, vt, g, wq, rrow, wout, *, heads, tq=512, grp=2):
    bsz, t, d = h.shape
    hk = wq.shape[1]
    nq, win = grp * CHUNK, (grp + N_PAST_CHUNKS) * CHUNK
    assert t % tq == 0 and tq % nq == 0 and 2 * (hk // heads) == LANES and nq == LANES
    assert ROLL_WIDTH <= win < 2 * ROLL_WIDTH
    tile_win = tq + BAND_PAST
    k_spec = pl.BlockSpec((pl.Element(1), pl.Element(tile_win), pl.Element(hk)),
                          lambda b, i: (b, i * tq, 0))
    vt_spec = pl.BlockSpec((pl.Element(1), pl.Element(tile_win // LANES), pl.Element(hk),
                            pl.Element(LANES)), lambda b, i: (b, i * (tq // LANES), 0, 0))
    return pl.pallas_call(
        functools.partial(_band_prompt_kernel, heads=heads, grp=grp),
        grid=(bsz, t // tq),
        in_specs=[pl.BlockSpec((1, tq, d), lambda b, i: (b, i, 0)),
                  _resident((1, d)), _resident(wq.shape), _resident(rrow.shape),
                  _resident(wout.shape), k_spec, vt_spec],
        out_specs=pl.BlockSpec((1, tq, d), lambda b, i: (b, i, 0)),
        out_shape=jax.ShapeDtypeStruct((bsz, t, d), F32),
        scratch_shapes=[pltpu.VMEM((heads // 2, win, 2 * nq), F32),
                        pltpu.VMEM((tq, hk), BF16),
                        pltpu.VMEM((tq, hk), BF16)],
        compiler_params=_params("arbitrary", "arbitrary"),
        name="band_prompt",
    )(h, g, wq, rrow, wout, kpad, vt)


def _band_sample_kernel(q_ref, ck_ref, cv_ref, kn_ref, vn_ref, rrow_ref, *out_refs,
                        heads, write_cache):
    if write_cache:
        o_ref, cko_ref, cvo_ref, bias_ref = out_refs
    else:
        o_ref, bias_ref = out_refs
    nq = q_ref.shape[1]
    npast = ck_ref.shape[1]
    nk = npast + nq

    pairs = range(heads // 2)
    cols = [slice(p * LANES, (p + 1) * LANES) for p in pairs]

    @pl.when(pl.program_id(0) == 0)
    def _():
        for hd in range(heads):
            bias_ref[hd // 2, (hd % 2) * nq:(hd % 2 + 1) * nq, :] = _toeplitz_bias(
                rrow_ref[hd:hd + 1, :], nq, nk)

    k_all = jnp.concatenate([ck_ref[0], kn_ref[0]], axis=0)
    v_all = jnp.concatenate([cv_ref[0], vn_ref[0]], axis=0)
    if write_cache:
        cko_ref[0] = k_all[nq:]
        cvo_ref[0] = v_all[nq:]
    k_bf, v_bf = k_all.astype(BF16), v_all.astype(BF16)
    q = q_ref[0]
    first = lax.broadcasted_iota(jnp.int32, (nq, LANES), 1) < (LANES // 2)
    scores = []
    for p in pairs:
        qp = q[:, cols[p]]
        zero = jnp.zeros_like(qp)
        q2 = jnp.concatenate([jnp.where(first, qp, zero), jnp.where(first, zero, qp)], axis=0)
        scores.append(_dot_nt(q2, k_bf[:, cols[p]]))
    probs, norms = [], []
    for p in pairs:
        s = scores[p] + bias_ref[p]
        e = jnp.exp(s - jnp.max(s, axis=-1, keepdims=True))
        norms.append(jnp.sum(e, axis=-1, keepdims=True))
        probs.append(e.astype(BF16))
    outs = []
    for p in pairs:
        o2 = _dot(probs[p], v_bf[:, cols[p]]) / norms[p]
        outs.append(jnp.where(first, o2[:nq], o2[nq:]))
    o_ref[0] = jnp.concatenate(outs, axis=1).astype(BF16)


def _band_sample(q, ck, cv, kn, vn, rrow, *, heads, write_cache):
    bsz, nq, hk = q.shape
    npast = ck.shape[1]
    assert 2 * (hk // heads) == LANES and npast + nq <= ROLL_WIDTH - nq
    seq = lambda n: pl.BlockSpec((1, n, hk), lambda b: (b, 0, 0))
    out_specs = [seq(nq)]
    out_shape = [jax.ShapeDtypeStruct((bsz, nq, hk), BF16)]
    if write_cache:
        out_specs += [seq(npast), seq(npast)]
        out_shape += [jax.ShapeDtypeStruct((bsz, npast, hk), F32)] * 2
    return pl.pallas_call(
        functools.partial(_band_sample_kernel, heads=heads, write_cache=write_cache),
        grid=(bsz,),
        in_specs=[seq(nq), seq(npast), seq(npast), seq(nq), seq(nq), _resident(rrow.shape)],
        out_specs=out_specs,
        out_shape=out_shape,
        scratch_shapes=[pltpu.VMEM((heads // 2, 2 * nq, npast + nq), F32)],
        compiler_params=_params("arbitrary"),
        name="band_sample",
    )(q, ck, cv, kn, vn, rrow)


def _bias_rows(table):
    clip = (table.shape[-1] - 1) // 2
    far = jnp.broadcast_to(table[..., -1:], table.shape[:-1] + (BAND_PAST - clip + 1,))
    near = jnp.flip(table[..., clip - CHUNK:-1], axis=-1)
    tail = jnp.broadcast_to(table[..., -1:], table.shape[:-1] + (ROLL_WIDTH - BAND - 1,))
    return jnp.concatenate([far, near, tail], axis=-1)


def kernel(x_prompt, x_sample, state_gla, cache_k, cache_v, norm_mix, norm_ffn, norm_final,
           gla_w_in, gla_w_gate2, gla_b_gate, gla_norm, gla_w_out, kv_norm, w_kv,
           attn_w_q, attn_rel_bias, attn_w_out, ffn_w_gate, ffn_w_up, ffn_w_down):
    bsz, seq, d = x_prompt.shape
    dbsz, dseq, _ = x_sample.shape
    n_a, _, heads_g, dk, dv = state_gla.shape
    depth = norm_mix.shape[0]
    _, npast, heads_a, dh = cache_k.shape
    hk = heads_a * dh
    nk, nv = heads_g * dk, heads_g * dv
    rank = gla_w_gate2.shape[1]
    assert seq % GLA_BLOCK == 0 and dseq % GLA_BLOCK != 0 and npast == BAND_PAST

    w_in = jnp.pad(gla_w_in, ((0, 0), (0, 0), (0, LANES - rank))).astype(BF16)
    w_g2 = jnp.pad(gla_w_gate2, ((0, 0), (0, LANES - rank), (0, 0))).astype(BF16)
    w_go = gla_w_out.astype(BF16)
    w_kvb = w_kv.astype(BF16)
    w_q = attn_w_q.astype(BF16)
    w_ao = attn_w_out.astype(BF16)
    w_fg, w_fu, w_fd = ffn_w_gate.astype(BF16), ffn_w_up.astype(BF16), ffn_w_down.astype(BF16)
    rrows = _bias_rows(attn_rel_bias)
    row = lambda a: a.reshape(1, -1)

    def ffn(h2, l):
        return _ffn(h2, row(norm_ffn[l]), w_fg[l], w_fu[l], w_fd[l], row(norm_final),
                    final=(l == depth - 1))

    h = x_prompt
    states = None
    for l in range(depth):
        if l < n_a:
            h, states = _gla_layer(h, (bsz, heads_g, dk, dv), row(norm_mix[l]), w_in[l], w_g2[l],
                                   row(gla_b_gate[l]), row(gla_norm[l]), w_go[l],
                                   blk=GLA_BLOCK, tt=512, chained=True,
                                   layer=l, n_layers=n_a, prev=states)
        else:
            if l == n_a:
                kpad, vt, k_last, v_last = _kv_prompt(h, row(kv_norm), w_kvb)
            i = l - n_a
            h = _band_prompt(h, kpad, vt, row(norm_mix[l]), w_q[i], rrows[i], w_ao[i],
                             heads=heads_a)
        h = ffn(h.reshape(bsz * seq, d), l).reshape(bsz, seq, d)
    y_prompt = h
    state_prompt = states.reshape(n_a, bsz, heads_g, dk, dv)
    ck_prompt = k_last.reshape(bsz, -1, heads_a, dh)
    cv_prompt = v_last.reshape(bsz, -1, heads_a, dh)

    n_s = dbsz * dseq
    h = x_sample.reshape(1, n_s, d)
    ck = cache_k.reshape(dbsz, npast, hk)
    cv = cache_v.reshape(dbsz, npast, hk)
    states = None
    for l in range(depth):
        if l < n_a:
            h, states = _gla_layer(h, state_gla[l], row(norm_mix[l]), w_in[l], w_g2[l],
                                   row(gla_b_gate[l]), row(gla_norm[l]), w_go[l],
                                   blk=dseq, tt=128, chained=False,
                                   layer=l, n_layers=n_a, prev=states)
            h2 = h.reshape(n_s, d)
        else:
            if l == n_a:
                kv_new = _norm_proj(h2, row(kv_norm), w_kvb)
                kn = kv_new[:, :hk].reshape(dbsz, dseq, hk)
                vn = kv_new[:, hk:].reshape(dbsz, dseq, hk)
            i = l - n_a
            q = _norm_proj(h2, row(norm_mix[l]), w_q[i], scale=dh ** -0.5, out_dtype=BF16)
            res = _band_sample(q.reshape(dbsz, dseq, hk), ck, cv, kn, vn, rrows[i],
                               heads=heads_a, write_cache=(l == n_a))
            if l == n_a:
                o, ck_new, cv_new = res
            else:
                o, = res
            h2 = _proj_res(o.reshape(n_s, hk), w_ao[i], h2)
        h2 = ffn(h2, l)
        h = h2.reshape(1, n_s, d)
    y_sample = h2.reshape(dbsz, dseq, d)
    state_sample = states.reshape(n_a, dbsz, heads_g, dk, dv)
    ck_sample = ck_new.reshape(dbsz, npast, heads_a, dh)
    cv_sample = cv_new.reshape(dbsz, npast, heads_a, dh)

    return (y_prompt, y_sample, state_prompt, ck_prompt, cv_prompt,
            state_sample, ck_sample, cv_sample)
```

```python
import functools

import jax
import jax.numpy as jnp
from jax import lax
from jax.experimental import pallas as pl
from jax.experimental.pallas import tpu as pltpu

F32 = jnp.float32
BF16 = jnp.bfloat16

EPS = 1e-6
CHUNK = 64
N_PAST_CHUNKS = 8
BAND_PAST = N_PAST_CHUNKS * CHUNK
BAND = BAND_PAST + CHUNK
NEG_INF = -1e30
LOG2E = 1.4426950408889634
GLA_TAU = 16.0
GLA_BLOCK = 64
LANES = 128
ROLL_WIDTH = 640
VMEM_LIMIT_BYTES = 56 * 1024 * 1024


def _params(*sem):
    return pltpu.CompilerParams(dimension_semantics=sem, vmem_limit_bytes=VMEM_LIMIT_BYTES)


def _resident(shape):
    nd = len(shape)
    return pl.BlockSpec(shape, lambda *_: (0,) * nd, pipeline_mode=pl.Buffered(1))


def _rms(x, g):
    ms = jnp.mean(x * x, axis=-1, keepdims=True)
    return x * lax.rsqrt(ms + EPS) * g


def _dot(a, b):
    return jnp.dot(a, b, preferred_element_type=F32)


def _dot_nt(a, b):
    return lax.dot_general(a, b, (((1,), (1,)), ((), ())), preferred_element_type=F32)


def _dot_tn(a, b):
    return lax.dot_general(a, b, (((0,), (0,)), ((), ())), preferred_element_type=F32)


def _split3(x):
    hi = x.astype(BF16)
    r1 = x - hi.astype(F32)
    mid = r1.astype(BF16)
    lo = (r1 - mid.astype(F32)).astype(BF16)
    return hi, mid, lo


def _log_sigmoid(x):
    return jnp.minimum(x, 0.0) - jnp.log(1.0 + jnp.exp(-jnp.abs(x)))


def _ffn_kernel(h_ref, g_ref, wg_ref, wu_ref, wd_ref, gf_ref, o_ref, hh_ref, *, tf, sub, final):
    n_ff = wg_ref.shape[1]
    for r in range(h_ref.shape[0] // sub):
        rows = slice(r * sub, (r + 1) * sub)
        h = h_ref[rows, :]
        xn = _rms(h, g_ref[...]).astype(BF16)
        for c in range(n_ff // tf):
            sl = slice(c * tf, (c + 1) * tf)
            a = _dot(xn, wg_ref[:, sl])
            u = _dot(xn, wu_ref[:, sl])
            hh_ref[rows, sl] = (a * jax.nn.sigmoid(a) * u).astype(BF16)
        y = h + _dot(hh_ref[rows, :], wd_ref[...])
        if final:
            y = _rms(y, gf_ref[...])
        o_ref[rows, :] = y


def _ffn(h, g, wg, wu, wd, gf, *, final, tm=1024, sub=256, tf=256):
    n, d = h.shape
    n_ff = wg.shape[1]
    tm = min(tm, n)
    sub = min(sub, tm)
    assert n % tm == 0 and tm % sub == 0 and n_ff % tf == 0
    return pl.pallas_call(
        functools.partial(_ffn_kernel, tf=tf, sub=sub, final=final),
        grid=(n // tm,),
        in_specs=[
            pl.BlockSpec((tm, d), lambda i: (i, 0)),
            _resident((1, d)),
            _resident((d, n_ff)),
            _resident((d, n_ff)),
            _resident((n_ff, d)),
            _resident((1, d)),
        ],
        out_specs=pl.BlockSpec((tm, d), lambda i: (i, 0)),
        out_shape=jax.ShapeDtypeStruct((n, d), F32),
        scratch_shapes=[pltpu.VMEM((tm, n_ff), BF16)],
        compiler_params=_params("arbitrary"),
        name="ffn",
    )(h, g, wg, wu, wd, gf)


def _gla_kernel(*refs, blk, sub, heads, dk, dv, chained, has_prev):
    h_ref, gm_ref, win_ref, wg2_ref, bg_ref, gn_ref, wout_ref = refs[:7]
    o_ref, s_ref, oacc_ref, upd_ref = refs[-4:]
    s_in_ref = s_ref if chained else refs[7]
    assert len(refs) == 11 + (not chained) + has_prev
    tt = h_ref.shape[1]
    nk, nv = heads * dk, heads * dv
    nblk = sub // blk
    kc = lambda hd: slice(hd * dk, (hd + 1) * dk)
    vc = lambda hd: slice(hd * dv, (hd + 1) * dv)
    tril = (lax.broadcasted_iota(jnp.int32, (blk, blk), 1)
            <= lax.broadcasted_iota(jnp.int32, (blk, blk), 0))

    if chained:
        @pl.when(pl.program_id(1) == 0)
        def _():
            s_ref[...] = jnp.zeros_like(s_ref)

    def project(r):
        base = r * sub
        h = h_ref[0, base:base + sub, :]
        xn = _rms(h, gm_ref[...]).astype(BF16)
        a = _dot(xn, win_ref[:, 2 * nk + 2 * nv:]).astype(BF16)
        logit = _dot(a, wg2_ref[...]) + bg_ref[...]
        qk = _dot(xn, win_ref[:, :2 * nk])
        q = qk[:, :nk] * (dk ** -0.5)
        k = qk[:, nk:]
        v = _dot(xn, win_ref[:, 2 * nk:2 * nk + nv]).astype(BF16)
        g = _dot(xn, win_ref[:, 2 * nk + nv:2 * nk + 2 * nv])
        la = _log_sigmoid(logit) * (1.0 / GLA_TAU)

        pos = lax.broadcasted_iota(jnp.int32, la.shape, 0) & (blk - 1)
        b = la
        step = 1
        while step < blk:
            b = b + jnp.where(pos >= step, pltpu.roll(b, step, 0), 0.0)
            step *= 2
        bt = jnp.concatenate(
            [jnp.broadcast_to(b[(j + 1) * blk - 1:(j + 1) * blk], (blk, nk)) for j in range(nblk)],
            axis=0)

        qd = (q * jnp.exp(b)).astype(BF16)
        kd = (k * jnp.exp(-b)).astype(BF16)
        kdec = (k * jnp.exp(bt - b)).astype(BF16)
        eb = jnp.exp(bt)
        return h, qd, kd, kdec, eb, v, g

    def mix(r, proj, s):
        base, jb = r * sub, r * nblk
        h, qd, kd, kdec, eb, v, g = proj
        items = [(j, hd) for j in range(nblk) for hd in range(heads)]
        rows = lambda j: slice(j * blk, (j + 1) * blk)
        trow = lambda j: slice(base + j * blk, base + (j + 1) * blk)

        att = {(j, hd): jnp.where(tril, _dot_nt(qd[rows(j), kc(hd)], kd[rows(j), kc(hd)]), 0.0)
               .astype(BF16) for j, hd in items}
        for j, hd in items:
            oacc_ref[trow(j), vc(hd)] = _dot(att[j, hd], v[rows(j), vc(hd)])
        for j, hd in items:
            upd_ref[jb + j, hd] = _dot_tn(kdec[rows(j), kc(hd)], v[rows(j), vc(hd)])

        for j, hd in items:
            s_j = s[hd] if chained else s_in_ref[jb + j, hd]
            oacc_ref[trow(j), vc(hd)] += _dot(qd[rows(j), kc(hd)], s_j.astype(BF16))
            eb_col = jnp.transpose(jnp.broadcast_to(eb[j * blk:j * blk + 1, kc(hd)], (dk, dk)))
            s_new = s_j * jnp.concatenate([eb_col] * (dv // dk), axis=1) + upd_ref[jb + j, hd]
            if chained:
                s = s[:hd] + [s_new] + s[hd + 1:]
            else:
                s_ref[jb + j, hd] = s_new

        o = oacc_ref[base:base + sub, :]
        gn = gn_ref[...]
        ys = []
        for hd in range(heads):
            oh = o[:, vc(hd)]
            ms = jnp.mean(oh * oh, axis=-1, keepdims=True)
            ys.append(oh * lax.rsqrt(ms + EPS) * gn[:, vc(hd)])
        y = jnp.concatenate(ys, axis=1)
        y = (y * (g * jax.nn.sigmoid(g))).astype(BF16)
        o_ref[0, base:base + sub, :] = h + _dot(y, wout_ref[...])
        return s

    s = [s_in_ref[0, hd] for hd in range(heads)] if chained else None
    nsub = tt // sub
    proj = project(0)
    for r in range(nsub):
        nxt = project(r + 1) if r + 1 < nsub else None
        s = mix(r, proj, s)
        proj = nxt
    if chained:
        for hd in range(heads):
            s_ref[0, hd] = s[hd]


def _gla_layer(h, s0, gm, win, wg2, bg, gn, wout, *, blk, tt, sub, chained, layer, n_layers,
               prev):
    bsz, t, d = h.shape
    heads, dk, dv = s0[1:] if chained else s0.shape[1:]
    nblk = tt // blk
    assert t % tt == 0 and tt % sub == 0 and sub % blk == 0
    weights = [gm, win, wg2, bg, gn, wout]
    in_specs = [pl.BlockSpec((1, tt, d), lambda b, i: (b, i, 0))]
    in_specs += [_resident(w.shape) for w in weights]
    args = [h] + weights
    if chained:
        nseq = bsz
        s_out_spec = pl.BlockSpec((1, heads, dk, dv), lambda b, i: (layer * bsz + b, 0, 0, 0))
    else:
        nseq = bsz * (t // blk)
        steps = t // tt
        in_specs.append(pl.BlockSpec((nblk, heads, dk, dv),
                                     lambda b, i: (b * steps + i, 0, 0, 0)))
        args.append(s0)
        s_out_spec = pl.BlockSpec((nblk, heads, dk, dv),
                                  lambda b, i: (layer * bsz * steps + b * steps + i, 0, 0, 0))
    aliases = {}
    if prev is not None:
        aliases = {len(args): 1}
        in_specs.append(pl.BlockSpec(memory_space=pl.ANY))
        args.append(prev)
    return pl.pallas_call(
        functools.partial(_gla_kernel, blk=blk, sub=sub, heads=heads, dk=dk, dv=dv,
                          chained=chained, has_prev=prev is not None),
        grid=(bsz, t // tt),
        in_specs=in_specs,
        out_specs=[pl.BlockSpec((1, tt, d), lambda b, i: (b, i, 0)), s_out_spec],
        out_shape=[jax.ShapeDtypeStruct((bsz, t, d), F32),
                   jax.ShapeDtypeStruct((n_layers * nseq, heads, dk, dv), F32)],
        input_output_aliases=aliases,
        scratch_shapes=[pltpu.VMEM((tt, heads * dv), F32),
                        pltpu.VMEM((nblk, heads, dk, dv), F32)],
        compiler_params=_params("arbitrary", "arbitrary"),
        name="gla_layer",
    )(*args)


def _norm_proj_kernel(h_ref, g_ref, w_ref, o_ref, *, scale):
    xn = _rms(h_ref[...], g_ref[...]).astype(BF16)
    y = _dot(xn, w_ref[...])
    if scale != 1.0:
        y = y * scale
    o_ref[...] = y.astype(o_ref.dtype)


def _norm_proj(h, g, w, *, scale=1.0, out_dtype=F32, tm=512):
    n, d = h.shape
    tm = min(tm, n)
    assert n % tm == 0
    return pl.pallas_call(
        functools.partial(_norm_proj_kernel, scale=scale),
        grid=(n // tm,),
        in_specs=[pl.BlockSpec((tm, d), lambda i: (i, 0)), _resident((1, d)), _resident(w.shape)],
        out_specs=pl.BlockSpec((tm, w.shape[1]), lambda i: (i, 0)),
        out_shape=jax.ShapeDtypeStruct((n, w.shape[1]), out_dtype),
        compiler_params=_params("arbitrary"),
        name="norm_proj",
    )(h, g, w)


def _proj_res_kernel(x_ref, w_ref, h_ref, o_ref):
    o_ref[...] = h_ref[...] + _dot(x_ref[...], w_ref[...])


def _proj_res(x, w, h, *, tm=512):
    n, d = h.shape
    tm = min(tm, n)
    assert n % tm == 0
    return pl.pallas_call(
        _proj_res_kernel,
        grid=(n // tm,),
        in_specs=[pl.BlockSpec((tm, x.shape[1]), lambda i: (i, 0)), _resident(w.shape),
                  pl.BlockSpec((tm, d), lambda i: (i, 0))],
        out_specs=pl.BlockSpec((tm, d), lambda i: (i, 0)),
        out_shape=jax.ShapeDtypeStruct((n, d), F32),
        compiler_params=_params("arbitrary"),
        name="proj_res",
    )(x, w, h)


def _kv_prompt_kernel(h_ref, g_ref, w_ref, kp_ref, vt_ref, kl_ref, vl_ref):
    j = pl.program_id(1)
    hk = kp_ref.shape[2]

    @pl.when(j == 0)
    def _():
        kp_ref[...] = jnp.zeros_like(kp_ref)
        vt_ref[...] = jnp.zeros_like(vt_ref)

    @pl.when(j > 0)
    def _():
        xn = _rms(h_ref[0], g_ref[...]).astype(BF16)
        zv = _dot(xn, w_ref[:, hk:])
        for jt in range(vt_ref.shape[1]):
            vt_ref[0, jt] = jnp.transpose(zv[jt * LANES:(jt + 1) * LANES, :]).astype(BF16)
        zk = _dot(xn, w_ref[:, :hk])
        kp_ref[0] = zk.astype(BF16)

        @pl.when(j == pl.num_programs(1) - 1)
        def _():
            kl_ref[0] = zk
            vl_ref[0] = zv


def _kv_prompt(h, g, w):
    bsz, t, d = h.shape
    hk = w.shape[1] // 2
    tt = BAND_PAST
    assert t % tt == 0 and tt % LANES == 0
    nt = t // tt
    last_spec = pl.BlockSpec((1, tt, hk), lambda b, j: (b, 0, 0))
    return pl.pallas_call(
        _kv_prompt_kernel,
        grid=(bsz, nt + 1),
        in_specs=[pl.BlockSpec((1, tt, d), lambda b, j: (b, jnp.maximum(j - 1, 0), 0)),
                  _resident((1, d)), _resident(w.shape)],
        out_specs=[pl.BlockSpec((1, tt, hk), lambda b, j: (b, j, 0)),
                   pl.BlockSpec((1, tt // LANES, hk, LANES), lambda b, j: (b, j, 0, 0)),
                   last_spec, last_spec],
        out_shape=[jax.ShapeDtypeStruct((bsz, t + tt, hk), BF16),
                   jax.ShapeDtypeStruct((bsz, (t + tt) // LANES, hk, LANES), BF16)]
                  + [jax.ShapeDtypeStruct((bsz, tt, hk), F32)] * 2,
        compiler_params=_params("arbitrary", "arbitrary"),
        name="kv_prompt",
    )(h, g, w)


def _toeplitz_bias(row, nq, nk):
    full = pltpu.roll(jnp.broadcast_to(row, (nq, ROLL_WIDTH)), 0, 1, stride=1, stride_axis=0)
    return full[:, :nk]


def _group_bias(row, grp):
    nq, win = grp * CHUNK, (grp + N_PAST_CHUNKS) * CHUNK
    full = pltpu.roll(jnp.broadcast_to(row, (nq, ROLL_WIDTH)), 0, 1, stride=1, stride_axis=0)
    if win > ROLL_WIDTH:
        full = jnp.concatenate([full, full[:, :win - ROLL_WIDTH]], axis=1)
    r = lax.broadcasted_iota(jnp.int32, (nq, win), 0)
    c = lax.broadcasted_iota(jnp.int32, (nq, win), 1)
    off = c - (r & -CHUNK)
    return jnp.where((off >= 0) & (off < BAND), full[:, :win], NEG_INF)


def _band_prompt_kernel(h_ref, g_ref, wq_ref, rrow_ref, wout_ref, k_ref, vt_ref, o_ref,
                        bias_ref, q_ref, oacc_ref, *, heads, grp, clip):
    t = pl.program_id(1)
    tq = h_ref.shape[1]
    nq, win = grp * CHUNK, (grp + N_PAST_CHUNKS) * CHUNK
    half = LANES // 2
    pairs = range(heads // 2)
    cols = [slice(p * LANES, (p + 1) * LANES) for p in pairs]

    far_lo, far_hi = (grp - 1) * CHUNK, BAND_PAST - clip

    @pl.when((pl.program_id(0) == 0) & (t == 0))
    def _():
        for p in pairs:
            rows = [rrow_ref[hd:hd + 1, :] for hd in (2 * p, 2 * p + 1)]
            bias_ref[p] = jnp.concatenate(
                [jnp.transpose(_group_bias((r - r[:, :1]) * LOG2E, grp)) for r in rows], axis=1)

    h = h_ref[0]
    xn = _rms(h, g_ref[...]).astype(BF16)
    dh = wq_ref.shape[1] // heads
    q_ref[...] = (_dot(xn, wq_ref[...]) * (dh ** -0.5 * LOG2E)).astype(BF16)

    first = lax.broadcasted_iota(jnp.int32, (nq, LANES), 1) < half

    def scores_of(gi):
        r0 = gi * nq
        scores = []
        for p in pairs:
            qp = q_ref[r0:r0 + nq, cols[p]]
            zero = jnp.zeros_like(qp)
            q2 = jnp.concatenate([jnp.where(first, qp, zero), jnp.where(first, zero, qp)], axis=0)
            scores.append(_dot_nt(k_ref[0, r0:r0 + win, cols[p]], q2))
        return scores

    def attend(gi, scores, masked):
        r0 = gi * nq
        probs, invs = [], []
        for p in pairs:
            s = scores[p]
            s = jnp.concatenate([s[:far_lo] + bias_ref[p, :far_lo, :], s[far_lo:far_hi],
                                 s[far_hi:] + bias_ref[p, far_hi:, :]], axis=0)
            if masked:
                key = lax.broadcasted_iota(jnp.int32, s.shape, 0)
                s = jnp.where(key >= BAND_PAST - gi * nq, s, NEG_INF)
            e = jnp.exp2(s - jnp.max(s, axis=0, keepdims=True))
            invs.append(1.0 / jnp.sum(e, axis=0, keepdims=True))
            probs.append(e.astype(BF16))
        for p in pairs:
            t0 = gi * (nq // LANES)
            vt = jnp.concatenate([vt_ref[0, t0 + j, cols[p], :] for j in range(win // LANES)],
                                 axis=1)
            o_t = _dot(vt, probs[p]) * invs[p]
            o_t = jnp.concatenate([o_t[:half, :nq], o_t[half:, nq:]], axis=0)
            oacc_ref[r0:r0 + nq, cols[p]] = jnp.transpose(o_t).astype(BF16)

    def run(masked):
        ngroups = tq // nq
        scores = scores_of(0)
        for gi in range(ngroups):
            nxt = scores_of(gi + 1) if gi + 1 < ngroups else None
            attend(gi, scores, masked)
            scores = nxt

    @pl.when(t == 0)
    def _():
        run(True)

    @pl.when(t != 0)
    def _():
        run(False)

    o_ref[0] = h + _dot(oacc_ref[...], wout_ref[...])


def _band_prompt(h, kpad, vt, g, wq, rrow, wout, *, heads, clip, tq=512, grp=2):
    bsz, t, d = h.shape
    hk = wq.shape[1]
    nq, win = grp * CHUNK, (grp + N_PAST_CHUNKS) * CHUNK
    assert t % tq == 0 and tq % nq == 0 and 2 * (hk // heads) == LANES and nq == LANES
    assert ROLL_WIDTH <= win < 2 * ROLL_WIDTH
    assert (grp - 1) * CHUNK < BAND_PAST - clip and clip % 8 == 0
    tile_win = tq + BAND_PAST
    k_spec = pl.BlockSpec((pl.Element(1), pl.Element(tile_win), pl.Element(hk)),
                          lambda b, i: (b, i * tq, 0))
    vt_spec = pl.BlockSpec((pl.Element(1), pl.Element(tile_win // LANES), pl.Element(hk),
                            pl.Element(LANES)), lambda b, i: (b, i * (tq // LANES), 0, 0))
    return pl.pallas_call(
        functools.partial(_band_prompt_kernel, heads=heads, grp=grp, clip=clip),
        grid=(bsz, t // tq),
        in_specs=[pl.BlockSpec((1, tq, d), lambda b, i: (b, i, 0)),
                  _resident((1, d)), _resident(wq.shape), _resident(rrow.shape),
                  _resident(wout.shape), k_spec, vt_spec],
        out_specs=pl.BlockSpec((1, tq, d), lambda b, i: (b, i, 0)),
        out_shape=jax.ShapeDtypeStruct((bsz, t, d), F32),
        scratch_shapes=[pltpu.VMEM((heads // 2, win, 2 * nq), F32),
                        pltpu.VMEM((tq, hk), BF16),
                        pltpu.VMEM((tq, hk), BF16)],
        compiler_params=_params("arbitrary", "arbitrary"),
        name="band_prompt",
    )(h, g, wq, rrow, wout, kpad, vt)


def _band_sample_kernel(q_ref, ck_ref, cv_ref, kn_ref, vn_ref, rrow_ref, *out_refs,
                        heads, write_cache):
    if write_cache:
        o_ref, cko_ref, cvo_ref, bias_ref = out_refs
    else:
        o_ref, bias_ref = out_refs
    nq = q_ref.shape[1]
    npast = ck_ref.shape[1]
    nk = npast + nq

    pairs = range(heads // 2)
    cols = [slice(p * LANES, (p + 1) * LANES) for p in pairs]

    @pl.when(pl.program_id(0) == 0)
    def _():
        for hd in range(heads):
            bias_ref[hd // 2, (hd % 2) * nq:(hd % 2 + 1) * nq, :] = _toeplitz_bias(
                rrow_ref[hd:hd + 1, :], nq, nk)

    k_all = jnp.concatenate([ck_ref[0], kn_ref[0]], axis=0)
    v_all = jnp.concatenate([cv_ref[0], vn_ref[0]], axis=0)
    if write_cache:
        cko_ref[0] = k_all[nq:]
        cvo_ref[0] = v_all[nq:]
    k_bf, v_bf = k_all.astype(BF16), v_all.astype(BF16)
    q = q_ref[0]
    first = lax.broadcasted_iota(jnp.int32, (nq, LANES), 1) < (LANES // 2)
    scores = []
    for p in pairs:
        qp = q[:, cols[p]]
        zero = jnp.zeros_like(qp)
        q2 = jnp.concatenate([jnp.where(first, qp, zero), jnp.where(first, zero, qp)], axis=0)
        scores.append(_dot_nt(q2, k_bf[:, cols[p]]))
    probs, norms = [], []
    for p in pairs:
        s = scores[p] + bias_ref[p]
        e = jnp.exp(s - jnp.max(s, axis=-1, keepdims=True))
        norms.append(jnp.sum(e, axis=-1, keepdims=True))
        probs.append(e.astype(BF16))
    outs = []
    for p in pairs:
        o2 = _dot(probs[p], v_bf[:, cols[p]]) / norms[p]
        outs.append(jnp.where(first, o2[:nq], o2[nq:]))
    o_ref[0] = jnp.concatenate(outs, axis=1).astype(BF16)


def _band_sample(q, ck, cv, kn, vn, rrow, *, heads, write_cache):
    bsz, nq, hk = q.shape
    npast = ck.shape[1]
    assert 2 * (hk // heads) == LANES and npast + nq <= ROLL_WIDTH - nq
    seq = lambda n: pl.BlockSpec((1, n, hk), lambda b: (b, 0, 0))
    out_specs = [seq(nq)]
    out_shape = [jax.ShapeDtypeStruct((bsz, nq, hk), BF16)]
    if write_cache:
        out_specs += [seq(npast), seq(npast)]
        out_shape += [jax.ShapeDtypeStruct((bsz, npast, hk), F32)] * 2
    return pl.pallas_call(
        functools.partial(_band_sample_kernel, heads=heads, write_cache=write_cache),
        grid=(bsz,),
        in_specs=[seq(nq), seq(npast), seq(npast), seq(nq), seq(nq), _resident(rrow.shape)],
        out_specs=out_specs,
        out_shape=out_shape,
        scratch_shapes=[pltpu.VMEM((heads // 2, 2 * nq, npast + nq), F32)],
        compiler_params=_params("arbitrary"),
        name="band_sample",
    )(q, ck, cv, kn, vn, rrow)


def _bias_rows(table):
    clip = (table.shape[-1] - 1) // 2
    far = jnp.broadcast_to(table[..., -1:], table.shape[:-1] + (BAND_PAST - clip + 1,))
    near = jnp.flip(table[..., clip - CHUNK:-1], axis=-1)
    tail = jnp.broadcast_to(table[..., -1:], table.shape[:-1] + (ROLL_WIDTH - BAND - 1,))
    return jnp.concatenate([far, near, tail], axis=-1)


def kernel(x_prompt, x_sample, state_gla, cache_k, cache_v, norm_mix, norm_ffn, norm_final,
           gla_w_in, gla_w_gate2, gla_b_gate, gla_norm, gla_w_out, kv_norm, w_kv,
           attn_w_q, attn_rel_bias, attn_w_out, ffn_w_gate, ffn_w_up, ffn_w_down):
    bsz, seq, d = x_prompt.shape
    dbsz, dseq, _ = x_sample.shape
    n_a, _, heads_g, dk, dv = state_gla.shape
    depth = norm_mix.shape[0]
    _, npast, heads_a, dh = cache_k.shape
    hk = heads_a * dh
    nk, nv = heads_g * dk, heads_g * dv
    rank = gla_w_gate2.shape[1]
    assert seq % GLA_BLOCK == 0 and dseq % GLA_BLOCK != 0 and npast == BAND_PAST

    w_in = jnp.pad(gla_w_in, ((0, 0), (0, 0), (0, LANES - rank))).astype(BF16)
    w_g2 = jnp.pad(gla_w_gate2, ((0, 0), (0, LANES - rank), (0, 0))).astype(BF16)
    w_go = gla_w_out.astype(BF16)
    w_kvb = w_kv.astype(BF16)
    w_q = attn_w_q.astype(BF16)
    w_ao = attn_w_out.astype(BF16)
    w_fg, w_fu, w_fd = ffn_w_gate.astype(BF16), ffn_w_up.astype(BF16), ffn_w_down.astype(BF16)
    rrows = _bias_rows(attn_rel_bias)
    row = lambda a: a.reshape(1, -1)

    def ffn(h2, l):
        return _ffn(h2, row(norm_ffn[l]), w_fg[l], w_fu[l], w_fd[l], row(norm_final),
                    final=(l == depth - 1))

    h = x_prompt
    states = None
    for l in range(depth):
        if l < n_a:
            h, states = _gla_layer(h, (bsz, heads_g, dk, dv), row(norm_mix[l]), w_in[l], w_g2[l],
                                   row(gla_b_gate[l]), row(gla_norm[l]), w_go[l],
                                   blk=GLA_BLOCK, tt=1024, sub=512, chained=True,
                                   layer=l, n_layers=n_a, prev=states)
        else:
            if l == n_a:
                kpad, vt, k_last, v_last = _kv_prompt(h, row(kv_norm), w_kvb)
            i = l - n_a
            h = _band_prompt(h, kpad, vt, row(norm_mix[l]), w_q[i], rrows[i], w_ao[i],
                             heads=heads_a, clip=(attn_rel_bias.shape[-1] - 1) // 2)
        h = ffn(h.reshape(bsz * seq, d), l).reshape(bsz, seq, d)
    y_prompt = h
    state_prompt = states.reshape(n_a, bsz, heads_g, dk, dv)
    ck_prompt = k_last.reshape(bsz, -1, heads_a, dh)
    cv_prompt = v_last.reshape(bsz, -1, heads_a, dh)

    n_s = dbsz * dseq
    h = x_sample.reshape(1, n_s, d)
    ck = cache_k.reshape(dbsz, npast, hk)
    cv = cache_v.reshape(dbsz, npast, hk)
    states = None
    for l in range(depth):
        if l < n_a:
            h, states = _gla_layer(h, state_gla[l], row(norm_mix[l]), w_in[l], w_g2[l],
                                   row(gla_b_gate[l]), row(gla_norm[l]), w_go[l],
                                   blk=dseq, tt=128, sub=128, chained=False,
                                   layer=l, n_layers=n_a, prev=states)
            h2 = h.reshape(n_s, d)
        else:
            if l == n_a:
                kv_new = _norm_proj(h2, row(kv_norm), w_kvb)
                kn = kv_new[:, :hk].reshape(dbsz, dseq, hk)
                vn = kv_new[:, hk:].reshape(dbsz, dseq, hk)
            i = l - n_a
            q = _norm_proj(h2, row(norm_mix[l]), w_q[i], scale=dh ** -0.5, out_dtype=BF16)
            res = _band_sample(q.reshape(dbsz, dseq, hk), ck, cv, kn, vn, rrows[i],
                               heads=heads_a, write_cache=(l == n_a))
            if l == n_a:
                o, ck_new, cv_new = res
            else:
                o, = res
            h2 = _proj_res(o.reshape(n_s, hk), w_ao[i], h2)
        h2 = ffn(h2, l)
        h = h2.reshape(1, n_s, d)
    y_sample = h2.reshape(dbsz, dseq, d)
    state_sample = states.reshape(n_a, dbsz, heads_g, dk, dv)
    ck_sample = ck_new.reshape(dbsz, npast, heads_a, dh)
    cv_sample = cv_new.reshape(dbsz, npast, heads_a, dh)

    return (y_prompt, y_sample, state_prompt, ck_prompt, cv_prompt,
            state_sample, ck_sample, cv_sample)
```

```python
import functools

import jax
import jax.numpy as jnp
from jax import lax
from jax.experimental import pallas as pl
from jax.experimental.pallas import tpu as pltpu

F32 = jnp.float32
BF16 = jnp.bfloat16

EPS = 1e-6
CHUNK = 64
N_PAST_CHUNKS = 8
BAND_PAST = N_PAST_CHUNKS * CHUNK
BAND = BAND_PAST + CHUNK
NEG_INF = -1e30
LOG2E = 1.4426950408889634
GLA_TAU = 16.0
GLA_BLOCK = 64
LANES = 128
ROLL_WIDTH = 640
VMEM_LIMIT_BYTES = 56 * 1024 * 1024


def _params(*sem):
    return pltpu.CompilerParams(dimension_semantics=sem, vmem_limit_bytes=VMEM_LIMIT_BYTES)


def _resident(shape):
    nd = len(shape)
    return pl.BlockSpec(shape, lambda *_: (0,) * nd, pipeline_mode=pl.Buffered(1))


def _rms(x, g):
    ms = jnp.mean(x * x, axis=-1, keepdims=True)
    return x * lax.rsqrt(ms + EPS) * g


def _dot(a, b):
    return jnp.dot(a, b, preferred_element_type=F32)


def _dot_nt(a, b):
    return lax.dot_general(a, b, (((1,), (1,)), ((), ())), preferred_element_type=F32)


def _dot_tn(a, b):
    return lax.dot_general(a, b, (((0,), (0,)), ((), ())), preferred_element_type=F32)


def _split3(x):
    hi = x.astype(BF16)
    r1 = x - hi.astype(F32)
    mid = r1.astype(BF16)
    lo = (r1 - mid.astype(F32)).astype(BF16)
    return hi, mid, lo


def _log_sigmoid(x):
    return jnp.minimum(x, 0.0) - jnp.log(1.0 + jnp.exp(-jnp.abs(x)))


def _ffn_kernel(h_ref, g_ref, wg_ref, wu_ref, wd_ref, gf_ref, o_ref, hh_ref, *, tf, sub, final):
    n_ff = wg_ref.shape[1]
    for r in range(h_ref.shape[0] // sub):
        rows = slice(r * sub, (r + 1) * sub)
        h = h_ref[rows, :]
        xn = _rms(h, g_ref[...]).astype(BF16)
        for c in range(n_ff // tf):
            sl = slice(c * tf, (c + 1) * tf)
            a = _dot(xn, wg_ref[:, sl])
            u = _dot(xn, wu_ref[:, sl])
            hh_ref[rows, sl] = (a * jax.nn.sigmoid(a) * u).astype(BF16)
        y = h + _dot(hh_ref[rows, :], wd_ref[...])
        if final:
            y = _rms(y, gf_ref[...])
        o_ref[rows, :] = y


def _ffn(h, g, wg, wu, wd, gf, *, final, tm=1024, sub=256, tf=256):
    n, d = h.shape
    n_ff = wg.shape[1]
    tm = min(tm, n)
    sub = min(sub, tm)
    assert n % tm == 0 and tm % sub == 0 and n_ff % tf == 0
    return pl.pallas_call(
        functools.partial(_ffn_kernel, tf=tf, sub=sub, final=final),
        grid=(n // tm,),
        in_specs=[
            pl.BlockSpec((tm, d), lambda i: (i, 0)),
            _resident((1, d)),
            _resident((d, n_ff)),
            _resident((d, n_ff)),
            _resident((n_ff, d)),
            _resident((1, d)),
        ],
        out_specs=pl.BlockSpec((tm, d), lambda i: (i, 0)),
        out_shape=jax.ShapeDtypeStruct((n, d), F32),
        scratch_shapes=[pltpu.VMEM((tm, n_ff), BF16)],
        compiler_params=_params("arbitrary"),
        name="ffn",
    )(h, g, wg, wu, wd, gf)


def _gla_kernel(*refs, blk, sub, heads, dk, dv, chained):
    h_ref, gm_ref, win_ref, wg2_ref, bg_ref, gn_ref, wout_ref = refs[:7]
    o_ref, s_ref, oacc_ref, upd_ref = refs[-4:]
    s_in_ref = s_ref if chained else refs[7]
    assert len(refs) == 12 + (not chained)
    tt = h_ref.shape[1]
    nk, nv = heads * dk, heads * dv
    nblk = sub // blk
    kc = lambda hd: slice(hd * dk, (hd + 1) * dk)
    vc = lambda hd: slice(hd * dv, (hd + 1) * dv)
    tril = (lax.broadcasted_iota(jnp.int32, (blk, blk), 1)
            <= lax.broadcasted_iota(jnp.int32, (blk, blk), 0))

    if chained:
        @pl.when(pl.program_id(1) == 0)
        def _():
            s_ref[...] = jnp.zeros_like(s_ref)

    def project(r):
        base = r * sub
        h = h_ref[0, base:base + sub, :]
        xn = _rms(h, gm_ref[...]).astype(BF16)
        a = _dot(xn, win_ref[:, 2 * nk + 2 * nv:]).astype(BF16)
        logit = _dot(a, wg2_ref[...]) + bg_ref[...]
        qk = _dot(xn, win_ref[:, :2 * nk])
        q = qk[:, :nk] * (dk ** -0.5)
        k = qk[:, nk:]
        v = _dot(xn, win_ref[:, 2 * nk:2 * nk + nv]).astype(BF16)
        g = _dot(xn, win_ref[:, 2 * nk + nv:2 * nk + 2 * nv])
        la = _log_sigmoid(logit) * (1.0 / GLA_TAU)

        pos = lax.broadcasted_iota(jnp.int32, la.shape, 0) & (blk - 1)
        b = la
        step = 1
        while step < blk:
            b = b + jnp.where(pos >= step, pltpu.roll(b, step, 0), 0.0)
            step *= 2
        bt = jnp.concatenate(
            [jnp.broadcast_to(b[(j + 1) * blk - 1:(j + 1) * blk], (blk, nk)) for j in range(nblk)],
            axis=0)

        qd = (q * jnp.exp(b)).astype(BF16)
        kd = (k * jnp.exp(-b)).astype(BF16)
        kdec = (k * jnp.exp(bt - b)).astype(BF16)
        eb = jnp.exp(bt)
        return h, qd, kd, kdec, eb, v, g

    def mix(r, proj, s):
        base, jb = r * sub, r * nblk
        h, qd, kd, kdec, eb, v, g = proj
        items = [(j, hd) for j in range(nblk) for hd in range(heads)]
        rows = lambda j: slice(j * blk, (j + 1) * blk)
        trow = lambda j: slice(base + j * blk, base + (j + 1) * blk)

        att = {(j, hd): jnp.where(tril, _dot_nt(qd[rows(j), kc(hd)], kd[rows(j), kc(hd)]), 0.0)
               .astype(BF16) for j, hd in items}
        for j, hd in items:
            upd_ref[jb + j, hd] = _dot_tn(kdec[rows(j), kc(hd)], v[rows(j), vc(hd)])

        for j, hd in items:
            s_j = s[hd] if chained else s_in_ref[jb + j, hd]
            oacc_ref[trow(j), vc(hd)] = _dot(
                jnp.concatenate([qd[rows(j), kc(hd)], att[j, hd]], axis=1),
                jnp.concatenate([s_j.astype(BF16), v[rows(j), vc(hd)]], axis=0))
            eb_col = jnp.transpose(jnp.broadcast_to(eb[j * blk:j * blk + 1, kc(hd)], (dk, dk)))
            s_new = s_j * jnp.concatenate([eb_col] * (dv // dk), axis=1) + upd_ref[jb + j, hd]
            if chained:
                s = s[:hd] + [s_new] + s[hd + 1:]
            else:
                s_ref[jb + j, hd] = s_new

        o = oacc_ref[base:base + sub, :]
        gn = gn_ref[...]
        ys = []
        for hd in range(heads):
            oh = o[:, vc(hd)]
            ms = jnp.mean(oh * oh, axis=-1, keepdims=True)
            ys.append(oh * lax.rsqrt(ms + EPS) * gn[:, vc(hd)])
        y = jnp.concatenate(ys, axis=1)
        y = (y * (g * jax.nn.sigmoid(g))).astype(BF16)
        o_ref[0, base:base + sub, :] = h + _dot(y, wout_ref[...])
        return s

    s = [s_in_ref[0, hd] for hd in range(heads)] if chained else None
    nsub = tt // sub
    proj = project(0)
    for r in range(nsub):
        nxt = project(r + 1) if r + 1 < nsub else None
        s = mix(r, proj, s)
        proj = nxt
    if chained:
        for hd in range(heads):
            s_ref[0, hd] = s[hd]


def _gla_layer(h, s0, gm, win, wg2, bg, gn, wout, *, blk, tt, sub, chained, layer, n_layers,
               prev):
    bsz, t, d = h.shape
    heads, dk, dv = s0[1:] if chained else s0.shape[1:]
    nblk = tt // blk
    assert t % tt == 0 and tt % sub == 0 and sub % blk == 0
    weights = [gm, win, wg2, bg, gn, wout]
    in_specs = [pl.BlockSpec((1, tt, d), lambda b, i: (b, i, 0))]
    in_specs += [_resident(w.shape) for w in weights]
    args = [h] + weights
    if chained:
        nseq = bsz
        s_out_spec = pl.BlockSpec((1, heads, dk, dv), lambda b, i: (layer * bsz + b, 0, 0, 0))
    else:
        nseq = bsz * (t // blk)
        steps = t // tt
        in_specs.append(pl.BlockSpec((nblk, heads, dk, dv),
                                     lambda b, i: (b * steps + i, 0, 0, 0)))
        args.append(s0)
        s_out_spec = pl.BlockSpec((nblk, heads, dk, dv),
                                  lambda b, i: (layer * bsz * steps + b * steps + i, 0, 0, 0))
    aliases = {len(args): 1}
    in_specs.append(pl.BlockSpec(memory_space=pl.ANY))
    args.append(prev)
    return pl.pallas_call(
        functools.partial(_gla_kernel, blk=blk, sub=sub, heads=heads, dk=dk, dv=dv,
                          chained=chained),
        grid=(bsz, t // tt),
        in_specs=in_specs,
        out_specs=[pl.BlockSpec((1, tt, d), lambda b, i: (b, i, 0)), s_out_spec],
        out_shape=[jax.ShapeDtypeStruct((bsz, t, d), F32),
                   jax.ShapeDtypeStruct((n_layers * nseq, heads, dk, dv), F32)],
        input_output_aliases=aliases,
        scratch_shapes=[pltpu.VMEM((tt, heads * dv), F32),
                        pltpu.VMEM((nblk, heads, dk, dv), F32)],
        compiler_params=_params("arbitrary", "arbitrary"),
        name="gla_layer",
    )(*args)


def _norm_proj_kernel(h_ref, g_ref, w_ref, o_ref, *, scale):
    xn = _rms(h_ref[...], g_ref[...]).astype(BF16)
    y = _dot(xn, w_ref[...])
    if scale != 1.0:
        y = y * scale
    o_ref[...] = y.astype(o_ref.dtype)


def _norm_proj(h, g, w, *, scale=1.0, out_dtype=F32, tm=512):
    n, d = h.shape
    tm = min(tm, n)
    assert n % tm == 0
    return pl.pallas_call(
        functools.partial(_norm_proj_kernel, scale=scale),
        grid=(n // tm,),
        in_specs=[pl.BlockSpec((tm, d), lambda i: (i, 0)), _resident((1, d)), _resident(w.shape)],
        out_specs=pl.BlockSpec((tm, w.shape[1]), lambda i: (i, 0)),
        out_shape=jax.ShapeDtypeStruct((n, w.shape[1]), out_dtype),
        compiler_params=_params("arbitrary"),
        name="norm_proj",
    )(h, g, w)


def _proj_res_kernel(x_ref, w_ref, h_ref, o_ref):
    o_ref[...] = h_ref[...] + _dot(x_ref[...], w_ref[...])


def _proj_res(x, w, h, *, tm=512):
    n, d = h.shape
    tm = min(tm, n)
    assert n % tm == 0
    return pl.pallas_call(
        _proj_res_kernel,
        grid=(n // tm,),
        in_specs=[pl.BlockSpec((tm, x.shape[1]), lambda i: (i, 0)), _resident(w.shape),
                  pl.BlockSpec((tm, d), lambda i: (i, 0))],
        out_specs=pl.BlockSpec((tm, d), lambda i: (i, 0)),
        out_shape=jax.ShapeDtypeStruct((n, d), F32),
        compiler_params=_params("arbitrary"),
        name="proj_res",
    )(x, w, h)


def _kv_prompt_kernel(h_ref, g_ref, w_ref, kp_ref, vt_ref, kl_ref, vl_ref):
    j = pl.program_id(1)
    hk = kp_ref.shape[2]

    @pl.when(j == 0)
    def _():
        kp_ref[...] = jnp.zeros_like(kp_ref)
        vt_ref[...] = jnp.zeros_like(vt_ref)

    @pl.when(j > 0)
    def _():
        xn = _rms(h_ref[0], g_ref[...]).astype(BF16)
        zv = _dot(xn, w_ref[:, hk:])
        for jt in range(vt_ref.shape[1]):
            vt_ref[0, jt] = jnp.transpose(zv[jt * LANES:(jt + 1) * LANES, :]).astype(BF16)
        zk = _dot(xn, w_ref[:, :hk])
        kp_ref[0] = zk.astype(BF16)

        @pl.when(j == pl.num_programs(1) - 1)
        def _():
            kl_ref[0] = zk
            vl_ref[0] = zv


def _kv_prompt(h, g, w):
    bsz, t, d = h.shape
    hk = w.shape[1] // 2
    tt = BAND_PAST
    assert t % tt == 0 and tt % LANES == 0
    nt = t // tt
    last_spec = pl.BlockSpec((1, tt, hk), lambda b, j: (b, 0, 0))
    return pl.pallas_call(
        _kv_prompt_kernel,
        grid=(bsz, nt + 1),
        in_specs=[pl.BlockSpec((1, tt, d), lambda b, j: (b, jnp.maximum(j - 1, 0), 0)),
                  _resident((1, d)), _resident(w.shape)],
        out_specs=[pl.BlockSpec((1, tt, hk), lambda b, j: (b, j, 0)),
                   pl.BlockSpec((1, tt // LANES, hk, LANES), lambda b, j: (b, j, 0, 0)),
                   last_spec, last_spec],
        out_shape=[jax.ShapeDtypeStruct((bsz, t + tt, hk), BF16),
                   jax.ShapeDtypeStruct((bsz, (t + tt) // LANES, hk, LANES), BF16)]
                  + [jax.ShapeDtypeStruct((bsz, tt, hk), F32)] * 2,
        compiler_params=_params("arbitrary", "arbitrary"),
        name="kv_prompt",
    )(h, g, w)


def _toeplitz_bias(row, nq, nk):
    full = pltpu.roll(jnp.broadcast_to(row, (nq, ROLL_WIDTH)), 0, 1, stride=1, stride_axis=0)
    return full[:, :nk]


def _group_bias(row, grp):
    nq, win = grp * CHUNK, (grp + N_PAST_CHUNKS) * CHUNK
    full = pltpu.roll(jnp.broadcast_to(row, (nq, ROLL_WIDTH)), 0, 1, stride=1, stride_axis=0)
    if win > ROLL_WIDTH:
        full = jnp.concatenate([full, full[:, :win - ROLL_WIDTH]], axis=1)
    r = lax.broadcasted_iota(jnp.int32, (nq, win), 0)
    c = lax.broadcasted_iota(jnp.int32, (nq, win), 1)
    off = c - (r & -CHUNK)
    return jnp.where((off >= 0) & (off < BAND), full[:, :win], NEG_INF)


def _band_prompt_kernel(h_ref, g_ref, wq_ref, rrow_ref, wout_ref, k_ref, vt_ref, o_ref,
                        bias_ref, q_ref, oacc_ref, *, heads, grp, clip):
    t = pl.program_id(1)
    tq = h_ref.shape[1]
    nq, win = grp * CHUNK, (grp + N_PAST_CHUNKS) * CHUNK
    half = LANES // 2
    pairs = range(heads // 2)
    cols = [slice(p * LANES, (p + 1) * LANES) for p in pairs]

    far_lo, far_hi = (grp - 1) * CHUNK, BAND_PAST - clip

    @pl.when((pl.program_id(0) == 0) & (t == 0))
    def _():
        for p in pairs:
            rows = [rrow_ref[hd:hd + 1, :] for hd in (2 * p, 2 * p + 1)]
            bias_ref[p] = jnp.concatenate(
                [jnp.transpose(_group_bias((r - r[:, :1]) * LOG2E, grp)) for r in rows], axis=1)

    dh = wq_ref.shape[1] // heads
    ngroups = tq // nq
    ghalf = ngroups // 2
    hrows = lambda hf: slice(hf * ghalf * nq, (hf + 1) * ghalf * nq)

    def q_proj(hf):
        xn = _rms(h_ref[0, hrows(hf), :], g_ref[...]).astype(BF16)
        q_ref[hrows(hf), :] = (_dot(xn, wq_ref[...]) * (dh ** -0.5 * LOG2E)).astype(BF16)

    def out_proj(hf):
        o_ref[0, hrows(hf), :] = h_ref[0, hrows(hf), :] + _dot(oacc_ref[hrows(hf), :], wout_ref[...])

    first = lax.broadcasted_iota(jnp.int32, (nq, LANES), 1) < half

    def scores_of(gi):
        r0 = gi * nq
        scores = []
        for p in pairs:
            qp = q_ref[r0:r0 + nq, cols[p]]
            zero = jnp.zeros_like(qp)
            q2 = jnp.concatenate([jnp.where(first, qp, zero), jnp.where(first, zero, qp)], axis=0)
            scores.append(_dot_nt(k_ref[0, r0:r0 + win, cols[p]], q2))
        return scores

    def attend(gi, scores, masked):
        r0 = gi * nq
        probs, invs = [], []
        for p in pairs:
            s = scores[p]
            s = jnp.concatenate([s[:far_lo] + bias_ref[p, :far_lo, :], s[far_lo:far_hi],
                                 s[far_hi:] + bias_ref[p, far_hi:, :]], axis=0)
            if masked:
                key = lax.broadcasted_iota(jnp.int32, s.shape, 0)
                s = jnp.where(key >= BAND_PAST - gi * nq, s, NEG_INF)
            e = jnp.exp2(s - jnp.max(s, axis=0, keepdims=True))
            invs.append(1.0 / jnp.sum(e, axis=0, keepdims=True))
            probs.append(e.astype(BF16))
        for p in pairs:
            t0 = gi * (nq // LANES)
            vt = jnp.concatenate([vt_ref[0, t0 + j, cols[p], :] for j in range(win // LANES)],
                                 axis=1)
            o_t = _dot(vt, probs[p]) * invs[p]
            o_t = jnp.concatenate([o_t[:half, :nq], o_t[half:, nq:]], axis=0)
            oacc_ref[r0:r0 + nq, cols[p]] = jnp.transpose(o_t).astype(BF16)

    def run(masked):
        q_proj(0)
        scores = scores_of(0)
        q_proj(1)
        for gi in range(ngroups):
            nxt = scores_of(gi + 1) if gi + 1 < ngroups else None
            attend(gi, scores, masked)
            if gi == ghalf - 1:
                out_proj(0)
            scores = nxt
        out_proj(1)

    @pl.when(t == 0)
    def _():
        run(True)

    @pl.when(t != 0)
    def _():
        run(False)


def _band_prompt(h, kpad, vt, g, wq, rrow, wout, *, heads, clip, tq=512, grp=2):
    bsz, t, d = h.shape
    hk = wq.shape[1]
    nq, win = grp * CHUNK, (grp + N_PAST_CHUNKS) * CHUNK
    assert t % tq == 0 and tq % nq == 0 and 2 * (hk // heads) == LANES and nq == LANES
    assert ROLL_WIDTH <= win < 2 * ROLL_WIDTH
    assert (grp - 1) * CHUNK < BAND_PAST - clip and clip % 8 == 0
    tile_win = tq + BAND_PAST
    k_spec = pl.BlockSpec((pl.Element(1), pl.Element(tile_win), pl.Element(hk)),
                          lambda b, i: (b, i * tq, 0))
    vt_spec = pl.BlockSpec((pl.Element(1), pl.Element(tile_win // LANES), pl.Element(hk),
                            pl.Element(LANES)), lambda b, i: (b, i * (tq // LANES), 0, 0))
    return pl.pallas_call(
        functools.partial(_band_prompt_kernel, heads=heads, grp=grp, clip=clip),
        grid=(bsz, t // tq),
        in_specs=[pl.BlockSpec((1, tq, d), lambda b, i: (b, i, 0)),
                  _resident((1, d)), _resident(wq.shape), _resident(rrow.shape),
                  _resident(wout.shape), k_spec, vt_spec],
        out_specs=pl.BlockSpec((1, tq, d), lambda b, i: (b, i, 0)),
        out_shape=jax.ShapeDtypeStruct((bsz, t, d), F32),
        scratch_shapes=[pltpu.VMEM((heads // 2, win, 2 * nq), F32),
                        pltpu.VMEM((tq, hk), BF16),
                        pltpu.VMEM((tq, hk), BF16)],
        compiler_params=_params("arbitrary", "arbitrary"),
        name="band_prompt",
    )(h, g, wq, rrow, wout, kpad, vt)


def _band_sample_kernel(q_ref, ck_ref, cv_ref, kn_ref, vn_ref, rrow_ref, *out_refs,
                        heads, write_cache):
    if write_cache:
        o_ref, cko_ref, cvo_ref, bias_ref = out_refs
    else:
        o_ref, bias_ref = out_refs
    nq = q_ref.shape[1]
    npast = ck_ref.shape[1]
    nk = npast + nq

    pairs = range(heads // 2)
    cols = [slice(p * LANES, (p + 1) * LANES) for p in pairs]

    @pl.when(pl.program_id(0) == 0)
    def _():
        for hd in range(heads):
            bias_ref[hd // 2, (hd % 2) * nq:(hd % 2 + 1) * nq, :] = _toeplitz_bias(
                rrow_ref[hd:hd + 1, :], nq, nk)

    k_all = jnp.concatenate([ck_ref[0], kn_ref[0]], axis=0)
    v_all = jnp.concatenate([cv_ref[0], vn_ref[0]], axis=0)
    if write_cache:
        cko_ref[0] = k_all[nq:]
        cvo_ref[0] = v_all[nq:]
    k_bf, v_bf = k_all.astype(BF16), v_all.astype(BF16)
    q = q_ref[0]
    first = lax.broadcasted_iota(jnp.int32, (nq, LANES), 1) < (LANES // 2)
    scores = []
    for p in pairs:
        qp = q[:, cols[p]]
        zero = jnp.zeros_like(qp)
        q2 = jnp.concatenate([jnp.where(first, qp, zero), jnp.where(first, zero, qp)], axis=0)
        scores.append(_dot_nt(q2, k_bf[:, cols[p]]))
    probs, norms = [], []
    for p in pairs:
        s = scores[p] + bias_ref[p]
        e = jnp.exp(s - jnp.max(s, axis=-1, keepdims=True))
        norms.append(jnp.sum(e, axis=-1, keepdims=True))
        probs.append(e.astype(BF16))
    outs = []
    for p in pairs:
        o2 = _dot(probs[p], v_bf[:, cols[p]]) / norms[p]
        outs.append(jnp.where(first, o2[:nq], o2[nq:]))
    o_ref[0] = jnp.concatenate(outs, axis=1).astype(BF16)


def _band_sample(q, ck, cv, kn, vn, rrow, *, heads, write_cache):
    bsz, nq, hk = q.shape
    npast = ck.shape[1]
    assert 2 * (hk // heads) == LANES and npast + nq <= ROLL_WIDTH - nq
    seq = lambda n: pl.BlockSpec((1, n, hk), lambda b: (b, 0, 0))
    out_specs = [seq(nq)]
    out_shape = [jax.ShapeDtypeStruct((bsz, nq, hk), BF16)]
    if write_cache:
        out_specs += [seq(npast), seq(npast)]
        out_shape += [jax.ShapeDtypeStruct((bsz, npast, hk), F32)] * 2
    return pl.pallas_call(
        functools.partial(_band_sample_kernel, heads=heads, write_cache=write_cache),
        grid=(bsz,),
        in_specs=[seq(nq), seq(npast), seq(npast), seq(nq), seq(nq), _resident(rrow.shape)],
        out_specs=out_specs,
        out_shape=out_shape,
        scratch_shapes=[pltpu.VMEM((heads // 2, 2 * nq, npast + nq), F32)],
        compiler_params=_params("arbitrary"),
        name="band_sample",
    )(q, ck, cv, kn, vn, rrow)


def _bias_rows(table):
    clip = (table.shape[-1] - 1) // 2
    far = jnp.broadcast_to(table[..., -1:], table.shape[:-1] + (BAND_PAST - clip + 1,))
    near = jnp.flip(table[..., clip - CHUNK:-1], axis=-1)
    tail = jnp.broadcast_to(table[..., -1:], table.shape[:-1] + (ROLL_WIDTH - BAND - 1,))
    return jnp.concatenate([far, near, tail], axis=-1)


def kernel(x_prompt, x_sample, state_gla, cache_k, cache_v, norm_mix, norm_ffn, norm_final,
           gla_w_in, gla_w_gate2, gla_b_gate, gla_norm, gla_w_out, kv_norm, w_kv,
           attn_w_q, attn_rel_bias, attn_w_out, ffn_w_gate, ffn_w_up, ffn_w_down):
    bsz, seq, d = x_prompt.shape
    dbsz, dseq, _ = x_sample.shape
    n_a, _, heads_g, dk, dv = state_gla.shape
    depth = norm_mix.shape[0]
    _, npast, heads_a, dh = cache_k.shape
    hk = heads_a * dh
    nk, nv = heads_g * dk, heads_g * dv
    rank = gla_w_gate2.shape[1]
    assert seq % GLA_BLOCK == 0 and dseq % GLA_BLOCK != 0 and npast == BAND_PAST

    w_in = jnp.pad(gla_w_in, ((0, 0), (0, 0), (0, LANES - rank))).astype(BF16)
    w_g2 = jnp.pad(gla_w_gate2, ((0, 0), (0, LANES - rank), (0, 0))).astype(BF16)
    w_go = gla_w_out.astype(BF16)
    w_kvb = w_kv.astype(BF16)
    w_q = attn_w_q.astype(BF16)
    w_ao = attn_w_out.astype(BF16)
    w_fg, w_fu, w_fd = ffn_w_gate.astype(BF16), ffn_w_up.astype(BF16), ffn_w_down.astype(BF16)
    rrows = _bias_rows(attn_rel_bias)
    row = lambda a: a.reshape(1, -1)

    def ffn(h2, l):
        return _ffn(h2, row(norm_ffn[l]), w_fg[l], w_fu[l], w_fd[l], row(norm_final),
                    final=(l == depth - 1))

    h = x_prompt
    states = jnp.zeros((n_a * bsz, heads_g, dk, dv), F32)
    for l in range(depth):
        if l < n_a:
            h, states = _gla_layer(h, (bsz, heads_g, dk, dv), row(norm_mix[l]), w_in[l], w_g2[l],
                                   row(gla_b_gate[l]), row(gla_norm[l]), w_go[l],
                                   blk=GLA_BLOCK, tt=1024, sub=512, chained=True,
                                   layer=l, n_layers=n_a, prev=states)
        else:
            if l == n_a:
                kpad, vt, k_last, v_last = _kv_prompt(h, row(kv_norm), w_kvb)
            i = l - n_a
            h = _band_prompt(h, kpad, vt, row(norm_mix[l]), w_q[i], rrows[i], w_ao[i],
                             heads=heads_a, clip=(attn_rel_bias.shape[-1] - 1) // 2)
        h = ffn(h.reshape(bsz * seq, d), l).reshape(bsz, seq, d)
    y_prompt = h
    state_prompt = states.reshape(n_a, bsz, heads_g, dk, dv)
    ck_prompt = k_last.reshape(bsz, -1, heads_a, dh)
    cv_prompt = v_last.reshape(bsz, -1, heads_a, dh)

    n_s = dbsz * dseq
    h = x_sample.reshape(1, n_s, d)
    ck = cache_k.reshape(dbsz, npast, hk)
    cv = cache_v.reshape(dbsz, npast, hk)
    states = jnp.zeros((n_a * dbsz, heads_g, dk, dv), F32)
    for l in range(depth):
        if l < n_a:
            h, states = _gla_layer(h, state_gla[l], row(norm_mix[l]), w_in[l], w_g2[l],
                                   row(gla_b_gate[l]), row(gla_norm[l]), w_go[l],
                                   blk=dseq, tt=128, sub=128, chained=False,
                                   layer=l, n_layers=n_a, prev=states)
            h2 = h.reshape(n_s, d)
        else:
            if l == n_a:
                kv_new = _norm_proj(h2, row(kv_norm), w_kvb)
                kn = kv_new[:, :hk].reshape(dbsz, dseq, hk)
                vn = kv_new[:, hk:].reshape(dbsz, dseq, hk)
            i = l - n_a
            q = _norm_proj(h2, row(norm_mix[l]), w_q[i], scale=dh ** -0.5, out_dtype=BF16)
            res = _band_sample(q.reshape(dbsz, dseq, hk), ck, cv, kn, vn, rrows[i],
                               heads=heads_a, write_cache=(l == n_a))
            if l == n_a:
                o, ck_new, cv_new = res
            else:
                o, = res
            h2 = _proj_res(o.reshape(n_s, hk), w_ao[i], h2)
        h2 = ffn(h2, l)
        h = h2.reshape(1, n_s, d)
    y_sample = h2.reshape(dbsz, dseq, d)
    state_sample = states.reshape(n_a, dbsz, heads_g, dk, dv)
    ck_sample = ck_new.reshape(dbsz, npast, heads_a, dh)
    cv_sample = cv_new.reshape(dbsz, npast, heads_a, dh)

    return (y_prompt, y_sample, state_prompt, ck_prompt, cv_prompt,
            state_sample, ck_sample, cv_sample)
```
